```python
import jax, jax.numpy as jnp
from jax import lax
import numpy as np

D_MODEL = 2048
BATCH = 2
SEQ = 8192
DEPTH = 2

M_HEADS = 4
M_HEAD_DIM = 256
W_M = M_HEADS * M_HEAD_DIM
SHORT_K = 3
CHUNK = 128
W_C = 1024
CONV_K = 31
D_FF = 5632
N_EXPERTS = 8
TOP_K = 2
D_FF_E = 7168
MOE_BLOCK = 256
N_DENSE = (DEPTH + 1) // 2
N_MOE = DEPTH // 2
EPS = 1e-6

OFF_Q = 0
OFF_K = W_M
OFF_V = 2 * W_M
OFF_O = 3 * W_M
OFF_G = 4 * W_M
OFF_GLU = OFF_G + 4 * M_HEADS
OFF_BR = OFF_GLU + 2 * W_C
N_IN = OFF_BR + 2 * D_MODEL

kernel_name = "hybrid_mlstm_conformer_moe_encoder"


def rms_norm(x):
    xf = x.astype(jnp.float32)
    return (xf * lax.rsqrt(jnp.mean(xf * xf, axis=-1, keepdims=True) + EPS)).astype(x.dtype)


def layer_norm(x, g, b):
    xf = x.astype(jnp.float32)
    mu = jnp.mean(xf, axis=-1, keepdims=True)
    var = jnp.mean(jnp.square(xf - mu), axis=-1, keepdims=True)
    y = (xf - mu) * lax.rsqrt(var + EPS)
    return (y * g.astype(jnp.float32) + b.astype(jnp.float32)).astype(x.dtype)


def depthwise_conv(x, w):
    pad = w.shape[0] // 2
    return lax.conv_general_dilated(x, w[:, None, :].astype(x.dtype), window_strides=(1,),
                                    padding=[(pad, pad)],
                                    dimension_numbers=('NWC', 'WIO', 'NWC'),
                                    feature_group_count=x.shape[-1])


def mlstm_scan(q, k, v, i_pre, f_pre):
    B, S, H, Dh = q.shape
    NC = S // CHUNK
    k = k * (Dh ** -0.5)
    logf = jax.nn.log_sigmoid(f_pre)

    def chunks(a):
        return a.reshape(B, NC, CHUNK, H, Dh).transpose(1, 0, 3, 2, 4)

    def gchunks(a):
        return a.reshape(B, NC, CHUNK, H).transpose(1, 0, 3, 2)

    qc, kc, vc = chunks(q), chunks(k), chunks(v)
    ic = gchunks(i_pre)
    bc = jnp.cumsum(gchunks(logf), axis=-1)
    tril = jnp.tril(jnp.ones((CHUNK, CHUNK), dtype=bool))

    def step(carry, xs):
        C, n, m = carry
        qb, kb, vb, ib, bb = xs
        dmat = bb[..., :, None] - bb[..., None, :] + ib[..., None, :]
        dmat = jnp.where(tril, dmat, -jnp.inf)
        inter = bb + m[..., None]
        m_t = jnp.maximum(inter, jnp.max(dmat, axis=-1))
        w_intra = jnp.exp(dmat - m_t[..., None])
        w_inter = jnp.exp(inter - m_t)
        s = jnp.einsum('bhtd,bhsd->bhts', qb, kb) * w_intra
        num = jnp.einsum('bhts,bhse->bhte', s, vb) + \
            w_inter[..., None] * jnp.einsum('bhtd,bhde->bhte', qb, C)
        den = jnp.sum(s, axis=-1) + w_inter * jnp.einsum('bhtd,bhd->bht', qb, n)
        h = num / jnp.maximum(jnp.abs(den), jnp.exp(-m_t))[..., None]
        m_new = m_t[..., -1]
        w_s = jnp.exp(bb[..., -1:] - bb + ib - m_new[..., None])
        carry_decay = jnp.exp(bb[..., -1] + m - m_new)
        kw = kb * w_s[..., None]
        C_new = carry_decay[..., None, None] * C + jnp.einsum('bhsd,bhse->bhde', kw, vb)
        n_new = carry_decay[..., None] * n + jnp.sum(kw, axis=2)
        return (C_new, n_new, m_new), h

    init = (jnp.zeros((B, H, Dh, Dh), jnp.float32), jnp.zeros((B, H, Dh), jnp.float32),
            jnp.zeros((B, H), jnp.float32))
    _, hs = lax.scan(step, init, (qc, kc, vc, ic, bc))
    return hs.transpose(1, 0, 3, 2, 4).reshape(B, S, H, Dh)


def mixer(h, w_in, b_in, w_qk_conv, m_norm_g, w_m_proj, w_dw, b_dw, ln_c_g, ln_c_b, w_c_proj, w_out):
    B, S, _ = h.shape
    z = h @ w_in + b_in
    qk = depthwise_conv(z[..., OFF_Q:OFF_V], w_qk_conv)
    q = qk[..., :W_M].reshape(B, S, M_HEADS, M_HEAD_DIM).astype(jnp.float32)
    k = qk[..., W_M:].reshape(B, S, M_HEADS, M_HEAD_DIM).astype(jnp.float32)
    v = z[..., OFF_V:OFF_O].reshape(B, S, M_HEADS, M_HEAD_DIM).astype(jnp.float32)
    o = z[..., OFF_O:OFF_G]
    gif = z[..., OFF_G:OFF_GLU].astype(jnp.float32).reshape(B, S, 2, 2, M_HEADS)
    h_fwd = mlstm_scan(q, k, v, gif[:, :, 0, 0], gif[:, :, 0, 1])
    rev = lambda a: jnp.flip(a, axis=1)
    h_bwd = rev(mlstm_scan(rev(q), rev(k), rev(v), rev(gif[:, :, 1, 0]), rev(gif[:, :, 1, 1])))
    hs = h_fwd + h_bwd
    hs = hs * lax.rsqrt(jnp.mean(hs * hs, axis=-1, keepdims=True) + EPS) * \
        m_norm_g.astype(jnp.float32).reshape(M_HEADS, M_HEAD_DIM)
    hm = hs.reshape(B, S, W_M).astype(h.dtype) * jax.nn.sigmoid(o)
    y_m = hm @ w_m_proj
    glu = z[..., OFF_GLU:OFF_BR]
    u = glu[..., :W_C] * jax.nn.sigmoid(glu[..., W_C:])
    u = depthwise_conv(u, w_dw) + b_dw
    u = jax.nn.silu(layer_norm(u, ln_c_g, ln_c_b))
    y_c = u @ w_c_proj
    gates = jax.nn.sigmoid(z[..., OFF_BR:])
    merged = gates[..., :D_MODEL] * y_m + gates[..., D_MODEL:] * y_c
    return merged @ w_out


def swiglu(h, w13, w2):
    a = h @ w13
    f = a.shape[-1] // 2
    return (jax.nn.silu(a[..., :f]) * a[..., f:]) @ w2


def moe_ffn(h, router, w13, w2):
    B, S, D = h.shape
    T = B * S
    A = T * TOP_K
    xt = h.reshape(T, D)
    logits = (xt @ router).astype(jnp.float32)
    top_v, top_e = lax.top_k(logits, TOP_K)
    gate_w = jax.nn.softmax(top_v, axis=-1)
    flat_e = top_e.reshape(-1)
    flat_tok = jnp.repeat(jnp.arange(T, dtype=jnp.int32), TOP_K)
    flat_w = gate_w.reshape(-1)
    order = jnp.argsort(flat_e, stable=True)
    se = flat_e[order]
    counts = jnp.zeros((N_EXPERTS,), jnp.int32).at[flat_e].add(1)
    padded = ((counts + MOE_BLOCK - 1) // MOE_BLOCK) * MOE_BLOCK
    pad_end = jnp.cumsum(padded)
    pad_start = pad_end - padded
    start = jnp.cumsum(counts) - counts
    dest = pad_start[se] + (jnp.arange(A, dtype=jnp.int32) - start[se])
    NB = (A + MOE_BLOCK - 1) // MOE_BLOCK + N_EXPERTS
    P = NB * MOE_BLOCK
    buf_tok = jnp.zeros((P,), jnp.int32).at[dest].set(flat_tok[order])
    buf_w = jnp.zeros((P,), jnp.float32).at[dest].set(flat_w[order])
    blk_e = jnp.minimum(jnp.searchsorted(pad_end, jnp.arange(NB, dtype=jnp.int32) * MOE_BLOCK,
                                         side='right'), N_EXPERTS - 1)
    xb = xt[buf_tok].reshape(NB, MOE_BLOCK, D)

    def expert_block(args):
        xblk, e = args
        return swiglu(xblk, w13[e], w2[e])

    yb = lax.map(expert_block, (xb, blk_e)).reshape(P, D)
    out = jnp.zeros((T, D), h.dtype).at[buf_tok].add(yb * buf_w[:, None].astype(yb.dtype))
    return out.reshape(B, S, D)


def setup_inputs(seed: int = 0) -> dict:
    key = jax.random.key(seed)
    ks = jax.random.split(key, 24)
    nrm = lambda k, shape, s: jax.random.normal(k, shape, jnp.float32) * s
    D = D_MODEL
    gk = jax.random.split(ks[4], 4)
    i_bias = nrm(gk[0], (DEPTH, 2, 1, M_HEADS), 0.1)
    f_bias = jnp.linspace(3.0, 6.0, M_HEADS, dtype=jnp.float32)[None, None, None, :] + \
        nrm(gk[1], (DEPTH, 2, 1, M_HEADS), 0.1)
    gif_bias = jnp.concatenate([i_bias, f_bias], axis=2).reshape(DEPTH, 4 * M_HEADS)
    b_in = jnp.concatenate([nrm(gk[2], (DEPTH, OFF_G), 0.02), gif_bias,
                            nrm(gk[3], (DEPTH, N_IN - OFF_GLU), 0.02)], axis=-1)
    return {
        "x": nrm(ks[0], (BATCH, SEQ, D), 1.0),
        "c": nrm(ks[1], (BATCH, D), 1.0),
        "w_mod": nrm(ks[2], (DEPTH, D, 6 * D), 0.5 * D ** -0.5),
        "b_mod": nrm(ks[3], (DEPTH, 6 * D), 0.02),
        "w_in": nrm(ks[5], (DEPTH, D, N_IN), D ** -0.5),
        "b_in": b_in,
        "w_qk_conv": nrm(ks[6], (DEPTH, SHORT_K, 2 * W_M), SHORT_K ** -0.5),
        "m_norm_g": 1.0 + nrm(ks[7], (DEPTH, W_M), 0.02),
        "w_m_proj": nrm(ks[8], (DEPTH, W_M, D), W_M ** -0.5),
        "w_dw": nrm(ks[9], (DEPTH, CONV_K, W_C), CONV_K ** -0.5),
        "b_dw": nrm(ks[10], (DEPTH, W_C), 0.02),
        "ln_c_g": 1.0 + nrm(ks[11], (DEPTH, W_C), 0.02),
        "ln_c_b": nrm(ks[12], (DEPTH, W_C), 0.02),
        "w_c_proj": nrm(ks[13], (DEPTH, W_C, D), W_C ** -0.5),
        "w_out": nrm(ks[14], (DEPTH, D, D), D ** -0.5),
        "ffn_w13": nrm(ks[15], (N_DENSE, D, 2 * D_FF), D ** -0.5),
        "ffn_w2": nrm(ks[16], (N_DENSE, D_FF, D), D_FF ** -0.5),
        "moe_router": nrm(ks[17], (N_MOE, D, N_EXPERTS), D ** -0.5),
        "moe_w13": nrm(ks[18], (N_MOE, N_EXPERTS, D, 2 * D_FF_E), D ** -0.5),
        "moe_w2": nrm(ks[19], (N_MOE, N_EXPERTS, D_FF_E, D), D_FF_E ** -0.5),
        "final_g": 1.0 + nrm(ks[20], (D,), 0.02),
    }


def reference(x, c, w_mod, b_mod, w_in, b_in, w_qk_conv, m_norm_g, w_m_proj, w_dw, b_dw,
              ln_c_g, ln_c_b, w_c_proj, w_out, ffn_w13, ffn_w2, moe_router, moe_w13, moe_w2,
              final_g):
    c_act = jax.nn.silu(c)
    for l in range(DEPTH):
        mod = c_act @ w_mod[l] + b_mod[l]
        sh1, sc1, g1, sh2, sc2, g2 = [m[:, None, :] for m in jnp.split(mod, 6, axis=-1)]
        h = rms_norm(x) * (1.0 + sc1) + sh1
        x = x + g1 * mixer(h, w_in[l], b_in[l], w_qk_conv[l], m_norm_g[l], w_m_proj[l],
                           w_dw[l], b_dw[l], ln_c_g[l], ln_c_b[l], w_c_proj[l], w_out[l])
        h = rms_norm(x) * (1.0 + sc2) + sh2
        if l % 2 == 0:
            f = swiglu(h, ffn_w13[l // 2], ffn_w2[l // 2])
        else:
            f = moe_ffn(h, moe_router[l // 2], moe_w13[l // 2], moe_w2[l // 2])
        x = x + g2 * f
    return rms_norm(x) * final_g
```

```python
import functools

import jax
import jax.numpy as jnp
from jax import lax
from jax.experimental import pallas as pl
from jax.experimental.pallas import tpu as pltpu

F32 = jnp.float32
BF16 = jnp.bfloat16
U32 = jnp.uint32
I32 = jnp.int32

HEAD_DIM = 256
CHUNK = 128
TOP_K = 2
EPS = 1e-6

V7X_VMEM_BYTES = 64 * 1024 * 1024
LANES = 128
SUBLANES = 8
BF16_TILE_ROWS = 16

HIGHEST = lax.Precision.HIGHEST
NEG_INF = float("-inf")


def _params(semantics, vmem_mib):
    assert vmem_mib * 1024 * 1024 < V7X_VMEM_BYTES
    return pltpu.CompilerParams(dimension_semantics=semantics, vmem_limit_bytes=vmem_mib * 1024 * 1024)


def _tile(n, pref):
    if n <= pref:
        return n
    t = pref - pref % LANES
    while t > LANES and n % t:
        t -= LANES
    assert n % t == 0, (n, pref)
    return t


def _sigmoid(x):
    return 1.0 / (1.0 + jnp.exp(-x))


def _silu(x):
    return x * _sigmoid(x)


def _log_sigmoid(x):
    return jnp.minimum(x, 0.0) - jnp.log1p(jnp.exp(-jnp.abs(x)))


def _rms(x):
    return x * lax.rsqrt(jnp.mean(x * x, axis=-1, keepdims=True) + EPS)


def _pack_halves(y):
    n = y.shape[-1] // 2
    lo = lax.bitcast_convert_type(y[:, :n].astype(BF16).astype(F32), U32)
    hi = lax.bitcast_convert_type(y[:, n:].astype(BF16).astype(F32), U32)
    return (hi & jnp.uint32(0xFFFF0000)) | (lo >> 16)


def _unpack_halves(u):
    lo = lax.bitcast_convert_type(u << 16, F32)
    hi = lax.bitcast_convert_type(u & jnp.uint32(0xFFFF0000), F32)
    return lo, hi


def _mod_kernel(c_ref, w_ref, b_ref, o_ref):
    c = c_ref[...]
    o_ref[0] = jnp.dot(_silu(c), w_ref[0], preferred_element_type=F32, precision=HIGHEST) + b_ref[0]


def _adaln_mod(c, w_mod, b_mod):
    depth, d, n = w_mod.shape
    b = c.shape[0]
    c_pad = jnp.zeros((SUBLANES, d), F32).at[:b].set(c)
    bn = _tile(n, 1024)
    out = pl.pallas_call(
        _mod_kernel,
        grid=(depth, n // bn),
        in_specs=[pl.BlockSpec((SUBLANES, d), lambda l, j: (0, 0)),
                  pl.BlockSpec((1, d, bn), lambda l, j: (l, 0, j)),
                  pl.BlockSpec((1, 1, bn), lambda l, j: (l, 0, j))],
        out_specs=pl.BlockSpec((1, SUBLANES, bn), lambda l, j: (l, 0, j)),
        out_shape=jax.ShapeDtypeStruct((depth, SUBLANES, n), F32),
        compiler_params=_params(("arbitrary", "arbitrary"), 40),
        name="adaln_mod",
    )(c_pad, w_mod, b_mod.reshape(depth, 1, n))
    return out[:, :b].reshape(depth, b, 6, d)


def _norm_kernel(x_ref, mod_ref, o_ref, *, shift, scale):
    x = x_ref[...]
    sh = mod_ref[0, shift:shift + 1, :]
    sc = mod_ref[0, scale:scale + 1, :]
    o_ref[...] = (_rms(x) * (1.0 + sc) + sh).astype(o_ref.dtype)


def _prenorm(x2, mod_l, seq, shift, scale):
    t, d = x2.shape
    bm = _tile(seq, 512)
    per_b = seq // bm
    return pl.pallas_call(
        functools.partial(_norm_kernel, shift=shift, scale=scale),
        grid=(t // bm,),
        in_specs=[pl.BlockSpec((bm, d), lambda i: (i, 0)),
                  pl.BlockSpec((1, 6, d), lambda i: (i // per_b, 0, 0))],
        out_specs=pl.BlockSpec((bm, d), lambda i: (i, 0)),
        out_shape=jax.ShapeDtypeStruct((t, d), BF16),
        compiler_params=_params(("parallel",), 32),
        name="prenorm",
    )(x2, mod_l)


def _mm_kernel(a_ref, w_ref, b_ref, o_ref):
    acc = jnp.dot(a_ref[...], w_ref[...], preferred_element_type=F32)
    o_ref[...] = (acc + b_ref[...]).astype(o_ref.dtype)


def _matmul_bias(a, w, bias, out_dtype, bm_pref=1024, bn_pref=1024):
    m, k = a.shape
    n = w.shape[1]
    bm = _tile(m, bm_pref)
    bn = _tile(n, bn_pref)
    return pl.pallas_call(
        _mm_kernel,
        grid=(n // bn, m // bm),
        in_specs=[pl.BlockSpec((bm, k), lambda j, i: (i, 0)),
                  pl.BlockSpec((k, bn), lambda j, i: (0, j)),
                  pl.BlockSpec((1, bn), lambda j, i: (0, j))],
        out_specs=pl.BlockSpec((bm, bn), lambda j, i: (i, j)),
        out_shape=jax.ShapeDtypeStruct((m, n), out_dtype),
        compiler_params=_params(("parallel", "arbitrary"), 48),
        name="matmul_bias",
    )(a, w, bias)


CONV_ROWS = 32
CONV_COLS = 256


def _fill_halo(xs_ref, prev, cur, nxt, first, last):
    hb = prev.shape[0]
    bm = cur.shape[0]
    xs_ref[0:hb, :] = jnp.where(first, 0.0, prev)
    xs_ref[hb:hb + bm, :] = cur
    xs_ref[hb + bm:hb + bm + hb, :] = jnp.where(last, 0.0, nxt)


def _conv_rows(xs_ref, w_ref, emit, *, taps, bm, width):
    pad = taps // 2
    lead = BF16_TILE_ROWS - pad
    span = CONV_ROWS + 2 * BF16_TILE_ROWS
    cb = min(CONV_COLS, width)
    assert bm % CONV_ROWS == 0 and width % cb == 0 and lead >= 0

    def body(rb, carry):
        r0 = pl.multiple_of(rb * CONV_ROWS, CONV_ROWS)
        for c0 in range(0, width, cb):
            blk = xs_ref[pl.ds(r0, span), c0:c0 + cb]
            acc = jnp.zeros((CONV_ROWS, cb), F32)
            for k in range(taps):
                acc = acc + w_ref[k:k + 1, c0:c0 + cb] * blk[lead + k:lead + k + CONV_ROWS, :]
            emit(r0, c0, acc)
        return carry

    lax.fori_loop(0, bm // CONV_ROWS, body, 0)


def _qkconv_kernel(prev_ref, cur_ref, next_ref, w_ref, o_ref, xs_ref, *, taps):
    i = pl.program_id(2)
    bm, width = cur_ref.shape[1], cur_ref.shape[2]
    _fill_halo(xs_ref, prev_ref[0].astype(F32), cur_ref[0].astype(F32), next_ref[0].astype(F32),
               i == 0, i == pl.num_programs(2) - 1)

    def emit(r0, c0, acc):
        o_ref[0, pl.ds(r0, CONV_ROWS), c0:c0 + acc.shape[1]] = acc.astype(o_ref.dtype)

    _conv_rows(xs_ref, w_ref, emit, taps=taps, bm=bm, width=width)


def _halo_specs(bm, cb, seq, col0):
    hb = BF16_TILE_ROWS
    per = bm // hb
    n_hb = seq // hb
    return [pl.BlockSpec((1, hb, cb), lambda b, j, i: (b, jnp.maximum(i * per - 1, 0), col0 + j)),
            pl.BlockSpec((1, bm, cb), lambda b, j, i: (b, i, col0 + j)),
            pl.BlockSpec((1, hb, cb), lambda b, j, i: (b, jnp.minimum((i + 1) * per, n_hb - 1), col0 + j))]


def _qk_conv(z3, w_qk, width):
    bsz, seq, _ = z3.shape
    taps = w_qk.shape[0]
    bm = _tile(seq, 512)
    cb = _tile(width, 512)
    return pl.pallas_call(
        functools.partial(_qkconv_kernel, taps=taps),
        grid=(bsz, width // cb, seq // bm),
        in_specs=_halo_specs(bm, cb, seq, 0) + [pl.BlockSpec((taps, cb), lambda b, j, i: (0, j))],
        out_specs=pl.BlockSpec((1, bm, cb), lambda b, j, i: (b, i, j)),
        out_shape=jax.ShapeDtypeStruct((bsz, seq, width), BF16),
        scratch_shapes=[pltpu.VMEM((bm + 2 * BF16_TILE_ROWS, cb), F32)],
        compiler_params=_params(("parallel", "parallel", "arbitrary"), 32),
        name="qk_conv",
    )(z3, z3, z3, w_qk)


def _conformer_kernel(pa_ref, ca_ref, na_ref, pb_ref, cb_ref, nb_ref, w_ref, bdw_ref, g_ref, b_ref,
                      o_ref, xs_ref, y_ref, *, taps):
    i = pl.program_id(2)
    bm, width = ca_ref.shape[1], ca_ref.shape[2]

    def glu(a_ref, b_ref_):
        return a_ref[0].astype(F32) * _sigmoid(b_ref_[0].astype(F32))

    _fill_halo(xs_ref, glu(pa_ref, pb_ref), glu(ca_ref, cb_ref), glu(na_ref, nb_ref),
               i == 0, i == pl.num_programs(2) - 1)

    def emit(r0, c0, acc):
        y_ref[pl.ds(r0, CONV_ROWS), c0:c0 + acc.shape[1]] = acc + bdw_ref[:, c0:c0 + acc.shape[1]]

    _conv_rows(xs_ref, w_ref, emit, taps=taps, bm=bm, width=width)
    y = y_ref[...]
    mu = jnp.mean(y, axis=-1, keepdims=True)
    yc = y - mu
    var = jnp.mean(yc * yc, axis=-1, keepdims=True)
    ln = yc * lax.rsqrt(var + EPS) * g_ref[...] + b_ref[...]
    o_ref[0] = _silu(ln).astype(o_ref.dtype)


def _conformer_branch(z3, col_a, w_dw, b_dw, ln_g, ln_b):
    bsz, seq, _ = z3.shape
    taps, wc = w_dw.shape
    assert col_a % wc == 0
    bm = _tile(seq, 256)
    blk_a = col_a // wc
    vec = lambda: pl.BlockSpec((1, wc), lambda b, j, i: (0, 0))
    return pl.pallas_call(
        functools.partial(_conformer_kernel, taps=taps),
        grid=(bsz, 1, seq // bm),
        in_specs=_halo_specs(bm, wc, seq, blk_a) + _halo_specs(bm, wc, seq, blk_a + 1)
        + [pl.BlockSpec((taps, wc), lambda b, j, i: (0, 0)), vec(), vec(), vec()],
        out_specs=pl.BlockSpec((1, bm, wc), lambda b, j, i: (b, i, 0)),
        out_shape=jax.ShapeDtypeStruct((bsz, seq, wc), BF16),
        scratch_shapes=[pltpu.VMEM((bm + 2 * BF16_TILE_ROWS, wc), F32), pltpu.VMEM((bm, wc), F32)],
        compiler_params=_params(("parallel", "parallel", "arbitrary"), 32),
        name="conformer_branch",
    )(z3, z3, z3, z3, z3, z3, w_dw, b_dw.reshape(1, wc), ln_g.reshape(1, wc), ln_b.reshape(1, wc))


def _mlstm_direction(q, k, v, i_col, lf_col, a_row, c_ref, n_ref, m_ref, tri_col, mask, last):
    length = q.shape[0]
    b_col = jnp.dot(tri_col, jnp.broadcast_to(lf_col, (length, LANES)),
                    preferred_element_type=F32, precision=HIGHEST)[:, 0:1]
    m_prev = m_ref[:, 0:1]
    dmat = jnp.where(mask, b_col - a_row, NEG_INF)
    inter = b_col + m_prev
    m_t = jnp.maximum(inter, jnp.max(dmat, axis=-1, keepdims=True))
    w_intra = jnp.exp(dmat - m_t)
    w_inter = jnp.exp(inter - m_t)
    s = lax.dot_general(q, k, (((1,), (1,)), ((), ())), preferred_element_type=F32) * w_intra
    c_prev = c_ref[...]
    n_prev = n_ref[...]
    num = jnp.dot(s.astype(BF16), v, preferred_element_type=F32) + \
        w_inter * jnp.dot(q, c_prev.astype(BF16), preferred_element_type=F32)
    den = jnp.sum(s, axis=-1, keepdims=True) + \
        w_inter * jnp.sum(q.astype(F32) * n_prev, axis=-1, keepdims=True)
    h = num / jnp.maximum(jnp.abs(den), jnp.exp(-m_t))
    m_new = m_t[last:last + 1, :]
    b_tot = b_col[last:last + 1, :]
    w_s = jnp.exp(b_tot - b_col + i_col - m_new)
    decay = jnp.exp(b_tot + m_prev - m_new)
    kw = k.astype(F32) * w_s
    c_ref[...] = decay * c_prev + lax.dot_general(kw.astype(BF16), v, (((0,), (0,)), ((), ())),
                                                  preferred_element_type=F32)
    n_ref[...] = decay * n_prev + jnp.sum(kw, axis=0, keepdims=True)
    m_ref[...] = jnp.broadcast_to(m_new, m_ref.shape)
    return h


def _mlstm_kernel(q_ref, k_ref, v_ref, o_ref, gcol_ref, grow_ref, g_ref, out_ref,
                  hs_ref, af_ref, ab_ref, cf_ref, cb_ref, nf_ref, nb_ref, mf_ref, mb_ref):
    length = CHUNK
    seq = q_ref.shape[1]
    n_chunks = seq // length
    row = lax.broadcasted_iota(I32, (length, length), 0)
    col = lax.broadcasted_iota(I32, (length, length), 1)
    lower = row >= col
    upper = row <= col
    lower_f = lower.astype(F32)
    upper_f = upper.astype(F32)

    gi_f, gf_f, gi_b, gf_b = (grow_ref[0, 0, g] for g in range(4))
    af_ref[...] = jnp.dot(_log_sigmoid(gf_f), upper_f, preferred_element_type=F32, precision=HIGHEST) - gi_f
    ab_ref[...] = jnp.dot(_log_sigmoid(gf_b), lower_f, preferred_element_type=F32, precision=HIGHEST) - gi_b

    for ref in (cf_ref, cb_ref, nf_ref, nb_ref, mf_ref, mb_ref):
        ref[...] = jnp.zeros(ref.shape, F32)

    def chunk_pair(c):
        cm = n_chunks - 1 - c
        rf = pl.multiple_of(c * length, length)
        rb = pl.multiple_of(cm * length, length)
        gf = gcol_ref[0, 0, pl.ds(rf, length), :]
        gb = gcol_ref[0, 0, pl.ds(rb, length), :]
        h_f = _mlstm_direction(q_ref[0, pl.ds(rf, length), :], k_ref[0, pl.ds(rf, length), :],
                               v_ref[0, pl.ds(rf, length), :], gf[:, 0:1], _log_sigmoid(gf[:, 1:2]),
                               af_ref[pl.ds(c, 1), :], cf_ref, nf_ref, mf_ref, lower_f, lower, length - 1)
        h_b = _mlstm_direction(q_ref[0, pl.ds(rb, length), :], k_ref[0, pl.ds(rb, length), :],
                               v_ref[0, pl.ds(rb, length), :], gb[:, 2:3], _log_sigmoid(gb[:, 3:4]),
                               ab_ref[pl.ds(cm, 1), :], cb_ref, nb_ref, mb_ref, upper_f, upper, 0)
        return rf, rb, h_f, h_b

    def first_touch(c, carry):
        rf, rb, h_f, h_b = chunk_pair(c)
        hs_ref[pl.ds(rf, length), :] = h_f
        hs_ref[pl.ds(rb, length), :] = h_b
        return carry

    def finish(r0, h):
        hs = hs_ref[pl.ds(r0, length), :] + h
        gate = _sigmoid(o_ref[0, pl.ds(r0, length), :].astype(F32))
        out_ref[0, pl.ds(r0, length), :] = ((_rms(hs) * g_ref[...]) * gate).astype(out_ref.dtype)

    def second_touch(c, carry):
        rf, rb, h_f, h_b = chunk_pair(c)
        finish(rf, h_f)
        finish(rb, h_b)
        return carry

    lax.fori_loop(0, n_chunks // 2, first_touch, 0)
    lax.fori_loop(n_chunks // 2, n_chunks, second_touch, 0)


def _mlstm_branch(qk3, z3, gates, m_norm_g, heads):
    bsz, seq, _ = z3.shape
    dh = HEAD_DIM
    wm = heads * dh
    n_chunks = seq // CHUNK
    assert seq % CHUNK == 0 and n_chunks % 2 == 0
    g5 = gates.reshape(bsz, seq, 2, 2, heads)
    gcol = g5.transpose(0, 4, 1, 2, 3).reshape(bsz, heads, seq, 4)
    grow = g5.transpose(0, 4, 2, 3, 1).reshape(bsz, heads, 4, n_chunks, CHUNK)
    once = pl.Buffered(1)
    seq_blk = lambda col0: pl.BlockSpec((1, seq, dh), lambda b, h: (b, 0, col0 + h), pipeline_mode=once)
    return pl.pallas_call(
        _mlstm_kernel,
        grid=(bsz, heads),
        in_specs=[pl.BlockSpec((1, seq, dh), lambda b, h: (b, 0, h), pipeline_mode=once),
                  pl.BlockSpec((1, seq, dh), lambda b, h: (b, 0, heads + h), pipeline_mode=once),
                  seq_blk(2 * heads), seq_blk(3 * heads),
                  pl.BlockSpec((1, 1, seq, 4), lambda b, h: (b, h, 0, 0), pipeline_mode=once),
                  pl.BlockSpec((1, 1, 4, n_chunks, CHUNK), lambda b, h: (b, h, 0, 0, 0)),
                  pl.BlockSpec((1, dh), lambda b, h: (0, h))],
        out_specs=pl.BlockSpec((1, seq, dh), lambda b, h: (b, 0, h)),
        out_shape=jax.ShapeDtypeStruct((bsz, seq, wm), BF16),
        scratch_shapes=[pltpu.VMEM((seq, dh), F32),
                        pltpu.VMEM((n_chunks, CHUNK), F32), pltpu.VMEM((n_chunks, CHUNK), F32),
                        pltpu.VMEM((dh, dh), F32), pltpu.VMEM((dh, dh), F32),
                        pltpu.VMEM((1, dh), F32), pltpu.VMEM((1, dh), F32),
                        pltpu.VMEM((1, LANES), F32), pltpu.VMEM((1, LANES), F32)],
        compiler_params=_params(("parallel", "parallel"), 52),
        name="mlstm",
    )(qk3, qk3, z3, z3, gcol, grow, m_norm_g.reshape(1, wm))


def _merge_kernel(hm_ref, uc_ref, gm_ref, gc_ref, x_ref, mod_ref, wm_ref, wc_ref, wo_ref, *rest, route):
    if route:
        rt_ref, xn_ref, hp_ref, lg_ref = rest
    else:
        xn_ref, hp_ref = rest
    y_m = jnp.dot(hm_ref[...], wm_ref[...], preferred_element_type=F32)
    y_c = jnp.dot(uc_ref[...], wc_ref[...], preferred_element_type=F32)
    merged = _sigmoid(gm_ref[...].astype(F32)) * y_m + _sigmoid(gc_ref[...].astype(F32)) * y_c
    out = jnp.dot(merged.astype(BF16), wo_ref[...], preferred_element_type=F32)
    x_new = x_ref[...] + mod_ref[0, 2:3, :] * out
    xn_ref[...] = x_new
    h = _rms(x_new) * (1.0 + mod_ref[0, 4:5, :]) + mod_ref[0, 3:4, :]
    hp_ref[...] = _pack_halves(h)
    if route:
        lg_ref[...] = lax.dot_general(rt_ref[...], h, (((1,), (1,)), ((), ())),
                                      preferred_element_type=F32, precision=HIGHEST)


def _merge(hm, uc, z2, col_gm, x2, mod_l, w_m, w_c, w_o, seq, router_t):
    t, d = x2.shape
    wm_, wc_ = hm.shape[1], uc.shape[1]
    bm = _tile(seq, 256)
    per_b = seq // bm
    assert col_gm % d == 0
    blk = col_gm // d
    route = router_t is not None
    once = pl.Buffered(1)
    in_specs = [pl.BlockSpec((bm, wm_), lambda i: (i, 0)),
                pl.BlockSpec((bm, wc_), lambda i: (i, 0)),
                pl.BlockSpec((bm, d), lambda i: (i, blk)),
                pl.BlockSpec((bm, d), lambda i: (i, blk + 1)),
                pl.BlockSpec((bm, d), lambda i: (i, 0)),
                pl.BlockSpec((1, 6, d), lambda i: (i // per_b, 0, 0)),
                pl.BlockSpec((wm_, d), lambda i: (0, 0), pipeline_mode=once),
                pl.BlockSpec((wc_, d), lambda i: (0, 0), pipeline_mode=once),
                pl.BlockSpec((d, d), lambda i: (0, 0), pipeline_mode=once)]
    out_specs = [pl.BlockSpec((bm, d), lambda i: (i, 0)), pl.BlockSpec((bm, d // 2), lambda i: (i, 0))]
    out_shape = [jax.ShapeDtypeStruct((t, d), F32), jax.ShapeDtypeStruct((t, d // 2), U32)]
    args = [hm, uc, z2, z2, x2, mod_l, w_m, w_c, w_o]
    if route:
        n_e = router_t.shape[0]
        in_specs.append(pl.BlockSpec((n_e, d), lambda i: (0, 0)))
        out_specs.append(pl.BlockSpec((n_e, bm), lambda i: (0, i)))
        out_shape.append(jax.ShapeDtypeStruct((n_e, t), F32))
        args.append(router_t)
    return pl.pallas_call(
        functools.partial(_merge_kernel, route=route),
        grid=(t // bm,),
        in_specs=in_specs, out_specs=out_specs, out_shape=out_shape,
        compiler_params=_params(("parallel",), 48),
        name="merge",
    )(*args)


FFN_ROWS = 512
FIRST_OF_GROUP = 1
VALID = 2


def _ffn_up_kernel(te_ref, tf_ref, x_ref, w1_ref, w3_ref, o_ref, wb_ref):
    r = pl.program_id(1)
    flag = tf_ref[r]

    @pl.when((flag & FIRST_OF_GROUP) != 0)
    def _():
        wb_ref[0] = w1_ref[0].astype(BF16)
        wb_ref[1] = w3_ref[0].astype(BF16)

    @pl.when((flag & VALID) != 0)
    def _():
        lo, hi = _unpack_halves(x_ref[...])
        x = jnp.concatenate([lo.astype(BF16), hi.astype(BF16)], axis=1)
        a = jnp.dot(x, wb_ref[0], preferred_element_type=F32)
        g = jnp.dot(x, wb_ref[1], preferred_element_type=F32)
        o_ref[...] = (_silu(a) * g).astype(o_ref.dtype)

    @pl.when((flag & VALID) == 0)
    def _():
        o_ref[...] = jnp.zeros(o_ref.shape, o_ref.dtype)


def _ffn_up(xp, w13, tile_e, tile_f):
    rows, dh = xp.shape
    n_e, d, f2 = w13.shape
    f = f2 // 2
    bm = FFN_ROWS
    bn = _tile(f, 512)
    nj = f // bn
    return pl.pallas_call(
        _ffn_up_kernel,
        grid_spec=pltpu.PrefetchScalarGridSpec(
            num_scalar_prefetch=2,
            grid=(nj, rows // bm),
            in_specs=[pl.BlockSpec((bm, dh), lambda j, r, te, tf: (r, 0)),
                      pl.BlockSpec((1, d, bn), lambda j, r, te, tf: (te[r], 0, j)),
                      pl.BlockSpec((1, d, bn), lambda j, r, te, tf: (te[r], 0, nj + j))],
            out_specs=pl.BlockSpec((bm, bn), lambda j, r, te, tf: (r, j)),
            scratch_shapes=[pltpu.VMEM((2, d, bn), BF16)]),
        out_shape=jax.ShapeDtypeStruct((rows, f), BF16),
        compiler_params=_params(("arbitrary", "arbitrary"), 48),
        name="ffn_up",
    )(tile_e, tile_f, xp, w13, w13)


def _ffn_down_kernel(te_ref, tf_ref, h_ref, wa_ref, wb_ref, o_ref):
    flag = tf_ref[pl.program_id(1)]

    @pl.when((flag & VALID) != 0)
    def _():
        h = h_ref[...]
        ya = jnp.dot(h, wa_ref[0], preferred_element_type=F32)
        yb = jnp.dot(h, wb_ref[0], preferred_element_type=F32)
        o_ref[...] = _pack_halves(jnp.concatenate([ya, yb], axis=1))

    @pl.when((flag & VALID) == 0)
    def _():
        o_ref[...] = jnp.zeros(o_ref.shape, o_ref.dtype)


def _ffn_down(h, w2, tile_e, tile_f):
    rows, f = h.shape
    n_e, _, d = w2.shape
    dh = d // 2
    bm = FFN_ROWS
    bn = _tile(dh, 256)
    nn = dh // bn
    return pl.pallas_call(
        _ffn_down_kernel,
        grid_spec=pltpu.PrefetchScalarGridSpec(
            num_scalar_prefetch=2,
            grid=(nn, rows // bm),
            in_specs=[pl.BlockSpec((bm, f), lambda n, r, te, tf: (r, 0)),
                      pl.BlockSpec((1, f, bn), lambda n, r, te, tf: (te[r], 0, n)),
                      pl.BlockSpec((1, f, bn), lambda n, r, te, tf: (te[r], 0, nn + n))],
            out_specs=pl.BlockSpec((bm, bn), lambda n, r, te, tf: (r, n))),
        out_shape=jax.ShapeDtypeStruct((rows, dh), U32),
        compiler_params=_params(("arbitrary", "arbitrary"), 48),
        name="ffn_down",
    )(tile_e, tile_f, h, w2, w2)


def _residual_epilogue(x, gate, f, mod_ref, fin_ref, xn_ref, hn_ref, final):
    x_new = x + gate * f
    if final:
        xn_ref[...] = _rms(x_new) * fin_ref[...]
    else:
        xn_ref[...] = x_new
        hn_ref[...] = (_rms(x_new) * (1.0 + mod_ref[0, 1:2, :]) + mod_ref[0, 0:1, :]).astype(hn_ref.dtype)


def _dense_residual_kernel(y_ref, x_ref, mod_ref, nmod_ref, fin_ref, xn_ref, *rest, final):
    hn_ref = None if final else rest[0]
    lo, hi = _unpack_halves(y_ref[...])
    f = jnp.concatenate([lo, hi], axis=1)
    _residual_epilogue(x_ref[...], mod_ref[0, 5:6, :], f, nmod_ref, fin_ref, xn_ref, hn_ref, final)


def _dense_residual(yp, x2, mod_l, mod_next, final_g, seq, final):
    t, d = x2.shape
    bm = _tile(seq, 512)
    per_b = seq // bm
    mod_spec = pl.BlockSpec((1, 6, d), lambda i: (i // per_b, 0, 0))
    row = lambda w: pl.BlockSpec((bm, w), lambda i: (i, 0))
    out_specs = [row(d)] + ([] if final else [row(d)])
    out_shape = [jax.ShapeDtypeStruct((t, d), F32)] + ([] if final else [jax.ShapeDtypeStruct((t, d), BF16)])
    return pl.pallas_call(
        functools.partial(_dense_residual_kernel, final=final),
        grid=(t // bm,),
        in_specs=[row(d // 2), row(d), mod_spec, mod_spec, pl.BlockSpec((1, d), lambda i: (0, 0))],
        out_specs=out_specs, out_shape=out_shape,
        compiler_params=_params(("parallel",), 40),
        name="dense_residual",
    )(yp, x2, mod_l, mod_next, final_g.reshape(1, d))


ROUTE_TOKENS = 512


def _route_kernel(lg_ref, idx_ref, wt_ref, cnt_ref, run_ref):
    step = pl.program_id(0)
    n_e, tb = lg_ref.shape

    @pl.when(step == 0)
    def _():
        run_ref[...] = jnp.zeros(run_ref.shape, F32)

    lg = lg_ref[...]
    eid = lax.broadcasted_iota(I32, (n_e, tb), 0)
    m1 = jnp.max(lg, axis=0, keepdims=True)
    e1 = jnp.min(jnp.where(lg == m1, eid, n_e), axis=0, keepdims=True)
    rest = jnp.where(eid == e1, NEG_INF, lg)
    m2 = jnp.max(rest, axis=0, keepdims=True)
    e2 = jnp.min(jnp.where(rest == m2, eid, n_e), axis=0, keepdims=True)
    p2 = jnp.exp(m2 - m1)
    w1 = 1.0 / (1.0 + p2)
    w2 = p2 / (1.0 + p2)
    member = jnp.where((eid == e1) | (eid == e2), 1.0, 0.0)
    before = (lax.broadcasted_iota(I32, (tb, tb), 0) < lax.broadcasted_iota(I32, (tb, tb), 1)).astype(BF16)
    rank = jnp.dot(member.astype(BF16), before, preferred_element_type=F32) + run_ref[:, 0:1]
    r1 = jnp.sum(jnp.where(eid == e1, rank, 0.0), axis=0, keepdims=True)
    r2 = jnp.sum(jnp.where(eid == e2, rank, 0.0), axis=0, keepdims=True)
    zero_i = jnp.zeros((SUBLANES - 4, tb), I32)
    idx_ref[...] = jnp.concatenate([e1, e2, r1.astype(I32), r2.astype(I32), zero_i], axis=0)
    wt_ref[...] = jnp.concatenate([w1, w2, jnp.zeros((SUBLANES - 2, tb), F32)], axis=0)
    run_ref[...] = run_ref[...] + jnp.sum(member, axis=1, keepdims=True)
    cnt_ref[...] = run_ref[...]


def _route(logits_t):
    n_e, t = logits_t.shape
    assert n_e % SUBLANES == 0
    tb = _tile(t, ROUTE_TOKENS)
    return pl.pallas_call(
        _route_kernel,
        grid=(t // tb,),
        in_specs=[pl.BlockSpec((n_e, tb), lambda i: (0, i))],
        out_specs=[pl.BlockSpec((SUBLANES, tb), lambda i: (0, i)),
                   pl.BlockSpec((SUBLANES, tb), lambda i: (0, i)),
                   pl.BlockSpec((n_e, LANES), lambda i: (0, 0))],
        out_shape=[jax.ShapeDtypeStruct((SUBLANES, t), I32), jax.ShapeDtypeStruct((SUBLANES, t), F32),
                   jax.ShapeDtypeStruct((n_e, LANES), F32)],
        scratch_shapes=[pltpu.VMEM((n_e, LANES), F32)],
        compiler_params=_params(("arbitrary",), 32),
        name="route",
    )(logits_t)


MOVE_TOKENS = 256


def _row_copy(src, dst, sem):
    return pltpu.make_async_copy(src, dst, sem)


def _dispatch_kernel(dest_ref, x_ref, init_ref, out_ref, sem):
    del init_ref
    n_tok = x_ref.shape[0]
    total = dest_ref.shape[0] // TOP_K
    base = pl.program_id(0) * n_tok

    def issue(i, carry):
        for k in range(TOP_K):
            _row_copy(x_ref.at[pl.ds(i, 1)], out_ref.at[pl.ds(dest_ref[k * total + base + i], 1)], sem).start()
        return carry

    def drain(i, carry):
        for k in range(TOP_K):
            _row_copy(x_ref.at[pl.ds(0, 1)], out_ref.at[pl.ds(0, 1)], sem).wait()
        return carry

    lax.fori_loop(0, n_tok, issue, 0)
    lax.fori_loop(0, n_tok, drain, 0)


def _dispatch(xp, dest_flat, rows):
    t, dh = xp.shape
    nt = _tile(t, MOVE_TOKENS)
    return pl.pallas_call(
        _dispatch_kernel,
        grid_spec=pltpu.PrefetchScalarGridSpec(
            num_scalar_prefetch=1,
            grid=(t // nt,),
            in_specs=[pl.BlockSpec((nt, dh), lambda i, dest: (i, 0)),
                      pl.BlockSpec(memory_space=pl.ANY)],
            out_specs=pl.BlockSpec(memory_space=pl.ANY),
            scratch_shapes=[pltpu.SemaphoreType.DMA(())]),
        out_shape=jax.ShapeDtypeStruct((rows, dh), U32),
        input_output_aliases={2: 0},
        compiler_params=_params(("arbitrary",), 32),
        name="dispatch",
    )(dest_flat, xp, jnp.zeros((rows, dh), U32))


def _combine_kernel(dest_ref, y_ref, wt_ref, x_ref, mod_ref, nmod_ref, fin_ref, xn_ref, *rest, final):
    if final:
        hn_ref = None
        buf_ref, sem = rest
    else:
        hn_ref, buf_ref, sem = rest
    n_tok = x_ref.shape[0]
    total = dest_ref.shape[0] // TOP_K
    base = pl.program_id(0) * n_tok

    def issue(i, carry):
        for k in range(TOP_K):
            _row_copy(y_ref.at[pl.ds(dest_ref[k * total + base + i], 1)], buf_ref.at[k, pl.ds(i, 1)], sem).start()
        return carry

    def drain(i, carry):
        for k in range(TOP_K):
            _row_copy(y_ref.at[pl.ds(0, 1)], buf_ref.at[0, pl.ds(0, 1)], sem).wait()
        return carry

    lax.fori_loop(0, n_tok, issue, 0)
    lax.fori_loop(0, n_tok, drain, 0)
    f = None
    for k in range(TOP_K):
        lo, hi = _unpack_halves(buf_ref[k])
        part = wt_ref[:, k:k + 1] * jnp.concatenate([lo, hi], axis=1)
        f = part if f is None else f + part
    _residual_epilogue(x_ref[...], mod_ref[0, 5:6, :], f, nmod_ref, fin_ref, xn_ref, hn_ref, final)


def _combine(yp, dest_flat, wt_cols, x2, mod_l, mod_next, final_g, seq, final):
    t, d = x2.shape
    dh = d // 2
    nt = _tile(seq, MOVE_TOKENS)
    per_b = seq // nt
    mod_spec = pl.BlockSpec((1, 6, d), lambda i, dest: (i // per_b, 0, 0))
    row = lambda w: pl.BlockSpec((nt, w), lambda i, dest: (i, 0))
    out_specs = [row(d)] + ([] if final else [row(d)])
    out_shape = [jax.ShapeDtypeStruct((t, d), F32)] + ([] if final else [jax.ShapeDtypeStruct((t, d), BF16)])
    return pl.pallas_call(
        functools.partial(_combine_kernel, final=final),
        grid_spec=pltpu.PrefetchScalarGridSpec(
            num_scalar_prefetch=1,
            grid=(t // nt,),
            in_specs=[pl.BlockSpec(memory_space=pl.ANY), row(TOP_K), row(d), mod_spec, mod_spec,
                      pl.BlockSpec((1, d), lambda i, dest: (0, 0))],
            out_specs=out_specs,
            scratch_shapes=[pltpu.VMEM((TOP_K, nt, dh), U32), pltpu.SemaphoreType.DMA(())]),
        out_shape=out_shape,
        compiler_params=_params(("arbitrary",), 40),
        name="combine",
    )(dest_flat, yp, wt_cols, x2, mod_l, mod_next, final_g.reshape(1, d))


def _dense_ffn(hp, x2, w13, w2, mod_l, mod_next, final_g, seq, final):
    t = hp.shape[0]
    assert t % FFN_ROWS == 0
    n_tiles = t // FFN_ROWS
    tile_e = jnp.zeros((n_tiles,), I32)
    tile_f = jnp.full((n_tiles,), VALID, I32).at[0].set(VALID | FIRST_OF_GROUP)
    hid = _ffn_up(hp, w13[None], tile_e, tile_f)
    yp = _ffn_down(hid, w2.astype(BF16)[None], tile_e, tile_f)
    return _dense_residual(yp, x2, mod_l, mod_next, final_g, seq, final)


def _moe_ffn(hp, logits_t, x2, w13, w2, mod_l, mod_next, final_g, seq, final):
    t = hp.shape[0]
    n_e = w13.shape[0]
    idx, wts, cnt = _route(logits_t)
    counts = cnt[:, 0].astype(I32)
    padded = ((counts + FFN_ROWS - 1) // FFN_ROWS) * FFN_ROWS
    seg_end = jnp.cumsum(padded)
    seg_start = seg_end - padded
    n_tiles = (t * TOP_K) // FFN_ROWS + n_e
    tile_row0 = jnp.arange(n_tiles, dtype=I32) * FFN_ROWS
    tile_e = jnp.minimum(jnp.searchsorted(seg_end, tile_row0, side="right"), n_e - 1).astype(I32)
    valid = tile_row0 < seg_end[-1]
    first = jnp.concatenate([jnp.ones((1,), bool), tile_e[1:] != tile_e[:-1]])
    tile_f = jnp.where(valid, VALID, 0) | jnp.where(first, FIRST_OF_GROUP, 0)
    dest = (seg_start[idx[:TOP_K]] + idx[TOP_K:2 * TOP_K]).reshape(-1).astype(I32)
    xs = _dispatch(hp, dest, n_tiles * FFN_ROWS)
    hid = _ffn_up(xs, w13, tile_e, tile_f.astype(I32))
    yp = _ffn_down(hid, w2.astype(BF16), tile_e, tile_f.astype(I32))
    return _combine(yp, dest, wts[:TOP_K].T, x2, mod_l, mod_next, final_g, seq, final)


def kernel(x, c, w_mod, b_mod, w_in, b_in, w_qk_conv, m_norm_g, w_m_proj, w_dw, b_dw, ln_c_g, ln_c_b,
           w_c_proj, w_out, ffn_w13, ffn_w2, moe_router, moe_w13, moe_w2, final_g):
    bsz, seq, d = x.shape
    depth = w_mod.shape[0]
    wm = w_m_proj.shape[1]
    wc = w_dw.shape[2]
    heads = wm // HEAD_DIM
    n_gate = 4 * heads
    off_g = 4 * wm
    off_glu = off_g + n_gate
    t = bsz * seq
    assert d % (2 * LANES) == 0 and n_gate <= LANES

    mods = _adaln_mod(c, w_mod, b_mod)
    x2 = x.reshape(t, d)
    h = _prenorm(x2, mods[0], seq, 0, 1)
    out = None
    for l in range(depth):
        final = l == depth - 1
        mod_l = mods[l]
        mod_next = mods[l] if final else mods[l + 1]
        w_main = jnp.concatenate([w_in[l][:, :off_g], w_in[l][:, off_glu:]], axis=1).astype(BF16)
        b_main = jnp.concatenate([b_in[l][:off_g], b_in[l][off_glu:]])[None]
        w_gate = jnp.zeros((d, LANES), BF16).at[:, :n_gate].set(w_in[l][:, off_g:off_glu].astype(BF16))
        b_gate = jnp.zeros((1, LANES), F32).at[0, :n_gate].set(b_in[l][off_g:off_glu])
        z = _matmul_bias(h, w_main, b_main, BF16)
        gates = _matmul_bias(h, w_gate, b_gate, F32, bn_pref=LANES)[:, :n_gate]
        z3 = z.reshape(bsz, seq, z.shape[1])
        k_scale = jnp.concatenate([jnp.ones((wm,), F32), jnp.full((wm,), HEAD_DIM ** -0.5, F32)])
        qk3 = _qk_conv(z3, w_qk_conv[l] * k_scale[None], 2 * wm)
        hm = _mlstm_branch(qk3, z3, gates.reshape(bsz, seq, n_gate), m_norm_g[l], heads)
        uc = _conformer_branch(z3, off_g, w_dw[l], b_dw[l], ln_c_g[l], ln_c_b[l])
        moe = l % 2 == 1
        router_t = moe_router[l // 2].T if moe else None
        res = _merge(hm.reshape(t, wm), uc.reshape(t, wc), z, off_g + 2 * wc, x2, mod_l,
                     w_m_proj[l].astype(BF16), w_c_proj[l].astype(BF16), w_out[l].astype(BF16), seq, router_t)
        if moe:
            x2, hp, logits_t = res
            res = _moe_ffn(hp, logits_t, x2, moe_w13[l // 2], moe_w2[l // 2], mod_l, mod_next, final_g,
                           seq, final)
        else:
            x2, hp = res
            res = _dense_ffn(hp, x2, ffn_w13[l // 2], ffn_w2[l // 2], mod_l, mod_next, final_g, seq, final)
        if final:
            out = res[0]
        else:
            x2, h = res
    return out.reshape(bsz, seq, d)
```

```python
import functools

import jax
import jax.numpy as jnp
from jax import lax
from jax.experimental import pallas as pl
from jax.experimental.pallas import tpu as pltpu

F32 = jnp.float32
BF16 = jnp.bfloat16
U32 = jnp.uint32
I32 = jnp.int32

HEAD_DIM = 256
CHUNK = 128
TOP_K = 2
EPS = 1e-6

V7X_VMEM_BYTES = 64 * 1024 * 1024
LANES = 128
SUBLANES = 8
BF16_TILE_ROWS = 16

HIGHEST = lax.Precision.HIGHEST
NEG_INF = float("-inf")


def _params(semantics, vmem_mib):
    assert vmem_mib * 1024 * 1024 < V7X_VMEM_BYTES
    return pltpu.CompilerParams(dimension_semantics=semantics, vmem_limit_bytes=vmem_mib * 1024 * 1024)


def _tile(n, pref):
    if n <= pref:
        return n
    t = pref - pref % LANES
    while t > LANES and n % t:
        t -= LANES
    assert n % t == 0, (n, pref)
    return t


def _sigmoid(x):
    return 1.0 / (1.0 + jnp.exp(-x))


def _silu(x):
    return x * _sigmoid(x)


def _log_sigmoid(x):
    return jnp.minimum(x, 0.0) - jnp.log1p(jnp.exp(-jnp.abs(x)))


def _rms(x):
    return x * lax.rsqrt(jnp.mean(x * x, axis=-1, keepdims=True) + EPS)


def _pack_halves(y):
    n = y.shape[-1] // 2
    lo = lax.bitcast_convert_type(y[:, :n].astype(BF16).astype(F32), U32)
    hi = lax.bitcast_convert_type(y[:, n:].astype(BF16).astype(F32), U32)
    return (hi & jnp.uint32(0xFFFF0000)) | (lo >> 16)


def _unpack_halves(u):
    lo = lax.bitcast_convert_type(u << 16, F32)
    hi = lax.bitcast_convert_type(u & jnp.uint32(0xFFFF0000), F32)
    return lo, hi


def _mod_kernel(c_ref, w_ref, b_ref, o_ref):
    c = c_ref[...]
    o_ref[0] = jnp.dot(_silu(c), w_ref[0], preferred_element_type=F32, precision=HIGHEST) + b_ref[0]


def _adaln_mod(c, w_mod, b_mod):
    depth, d, n = w_mod.shape
    b = c.shape[0]
    c_pad = jnp.zeros((SUBLANES, d), F32).at[:b].set(c)
    bn = _tile(n, 1024)
    out = pl.pallas_call(
        _mod_kernel,
        grid=(depth, n // bn),
        in_specs=[pl.BlockSpec((SUBLANES, d), lambda l, j: (0, 0)),
                  pl.BlockSpec((1, d, bn), lambda l, j: (l, 0, j)),
                  pl.BlockSpec((1, 1, bn), lambda l, j: (l, 0, j))],
        out_specs=pl.BlockSpec((1, SUBLANES, bn), lambda l, j: (l, 0, j)),
        out_shape=jax.ShapeDtypeStruct((depth, SUBLANES, n), F32),
        compiler_params=_params(("arbitrary", "arbitrary"), 40),
        name="adaln_mod",
    )(c_pad, w_mod, b_mod.reshape(depth, 1, n))
    return out[:, :b].reshape(depth, b, 6, d)


def _norm_kernel(x_ref, mod_ref, o_ref, *, shift, scale):
    x = x_ref[...]
    sh = mod_ref[0, shift:shift + 1, :]
    sc = mod_ref[0, scale:scale + 1, :]
    o_ref[...] = (_rms(x) * (1.0 + sc) + sh).astype(o_ref.dtype)


def _prenorm(x2, mod_l, seq, shift, scale):
    t, d = x2.shape
    bm = _tile(seq, 512)
    per_b = seq // bm
    return pl.pallas_call(
        functools.partial(_norm_kernel, shift=shift, scale=scale),
        grid=(t // bm,),
        in_specs=[pl.BlockSpec((bm, d), lambda i: (i, 0)),
                  pl.BlockSpec((1, 6, d), lambda i: (i // per_b, 0, 0))],
        out_specs=pl.BlockSpec((bm, d), lambda i: (i, 0)),
        out_shape=jax.ShapeDtypeStruct((t, d), BF16),
        compiler_params=_params(("parallel",), 32),
        name="prenorm",
    )(x2, mod_l)


def _mm_kernel(a_ref, w_ref, b_ref, o_ref):
    acc = jnp.dot(a_ref[...], w_ref[...], preferred_element_type=F32)
    o_ref[...] = (acc + b_ref[...]).astype(o_ref.dtype)


def _matmul_bias(a, w, bias, out_dtype, bm_pref=1024, bn_pref=1024):
    m, k = a.shape
    n = w.shape[1]
    bm = _tile(m, bm_pref)
    bn = _tile(n, bn_pref)
    return pl.pallas_call(
        _mm_kernel,
        grid=(n // bn, m // bm),
        in_specs=[pl.BlockSpec((bm, k), lambda j, i: (i, 0)),
                  pl.BlockSpec((k, bn), lambda j, i: (0, j)),
                  pl.BlockSpec((1, bn), lambda j, i: (0, j))],
        out_specs=pl.BlockSpec((bm, bn), lambda j, i: (i, j)),
        out_shape=jax.ShapeDtypeStruct((m, n), out_dtype),
        compiler_params=_params(("parallel", "arbitrary"), 48),
        name="matmul_bias",
    )(a, w, bias)


CONV_ROWS = 32
CONV_COLS = 256


def _fill_halo(xs_ref, prev, cur, nxt, first, last):
    hb = prev.shape[0]
    bm = cur.shape[0]
    xs_ref[0:hb, :] = jnp.where(first, 0.0, prev)
    xs_ref[hb:hb + bm, :] = cur
    xs_ref[hb + bm:hb + bm + hb, :] = jnp.where(last, 0.0, nxt)


def _conv_rows(xs_ref, w_ref, emit, *, taps, bm, width):
    pad = taps // 2
    lead = BF16_TILE_ROWS - pad
    span = CONV_ROWS + 2 * BF16_TILE_ROWS
    cb = min(CONV_COLS, width)
    assert bm % CONV_ROWS == 0 and width % cb == 0 and lead >= 0

    def body(rb, carry):
        r0 = pl.multiple_of(rb * CONV_ROWS, CONV_ROWS)
        for c0 in range(0, width, cb):
            blk = xs_ref[pl.ds(r0, span), c0:c0 + cb]
            acc = jnp.zeros((CONV_ROWS, cb), F32)
            for k in range(taps):
                acc = acc + w_ref[k:k + 1, c0:c0 + cb] * blk[lead + k:lead + k + CONV_ROWS, :]
            emit(r0, c0, acc)
        return carry

    lax.fori_loop(0, bm // CONV_ROWS, body, 0)


def _qkconv_kernel(prev_ref, cur_ref, next_ref, w_ref, o_ref, xs_ref, *, taps):
    i = pl.program_id(2)
    bm, width = cur_ref.shape[1], cur_ref.shape[2]
    _fill_halo(xs_ref, prev_ref[0].astype(F32), cur_ref[0].astype(F32), next_ref[0].astype(F32),
               i == 0, i == pl.num_programs(2) - 1)

    def emit(r0, c0, acc):
        o_ref[0, pl.ds(r0, CONV_ROWS), c0:c0 + acc.shape[1]] = acc.astype(o_ref.dtype)

    _conv_rows(xs_ref, w_ref, emit, taps=taps, bm=bm, width=width)


def _halo_specs(bm, cb, seq, col0):
    hb = BF16_TILE_ROWS
    per = bm // hb
    n_hb = seq // hb
    return [pl.BlockSpec((1, hb, cb), lambda b, j, i: (b, jnp.maximum(i * per - 1, 0), col0 + j)),
            pl.BlockSpec((1, bm, cb), lambda b, j, i: (b, i, col0 + j)),
            pl.BlockSpec((1, hb, cb), lambda b, j, i: (b, jnp.minimum((i + 1) * per, n_hb - 1), col0 + j))]


def _qk_conv(z3, w_qk, width):
    bsz, seq, _ = z3.shape
    taps = w_qk.shape[0]
    bm = _tile(seq, 512)
    cb = _tile(width, 512)
    return pl.pallas_call(
        functools.partial(_qkconv_kernel, taps=taps),
        grid=(bsz, width // cb, seq // bm),
        in_specs=_halo_specs(bm, cb, seq, 0) + [pl.BlockSpec((taps, cb), lambda b, j, i: (0, j))],
        out_specs=pl.BlockSpec((1, bm, cb), lambda b, j, i: (b, i, j)),
        out_shape=jax.ShapeDtypeStruct((bsz, seq, width), BF16),
        scratch_shapes=[pltpu.VMEM((bm + 2 * BF16_TILE_ROWS, cb), F32)],
        compiler_params=_params(("parallel", "parallel", "arbitrary"), 32),
        name="qk_conv",
    )(z3, z3, z3, w_qk)


def _conformer_kernel(pa_ref, ca_ref, na_ref, pb_ref, cb_ref, nb_ref, w_ref, bdw_ref, g_ref, b_ref,
                      o_ref, xs_ref, y_ref, *, taps):
    i = pl.program_id(2)
    bm, width = ca_ref.shape[1], ca_ref.shape[2]

    def glu(a_ref, b_ref_):
        return a_ref[0].astype(F32) * _sigmoid(b_ref_[0].astype(F32))

    _fill_halo(xs_ref, glu(pa_ref, pb_ref), glu(ca_ref, cb_ref), glu(na_ref, nb_ref),
               i == 0, i == pl.num_programs(2) - 1)

    def emit(r0, c0, acc):
        y_ref[pl.ds(r0, CONV_ROWS), c0:c0 + acc.shape[1]] = acc + bdw_ref[:, c0:c0 + acc.shape[1]]

    _conv_rows(xs_ref, w_ref, emit, taps=taps, bm=bm, width=width)
    y = y_ref[...]
    mu = jnp.mean(y, axis=-1, keepdims=True)
    yc = y - mu
    var = jnp.mean(yc * yc, axis=-1, keepdims=True)
    ln = yc * lax.rsqrt(var + EPS) * g_ref[...] + b_ref[...]
    o_ref[0] = _silu(ln).astype(o_ref.dtype)


def _conformer_branch(z3, col_a, w_dw, b_dw, ln_g, ln_b):
    bsz, seq, _ = z3.shape
    taps, wc = w_dw.shape
    assert col_a % wc == 0
    bm = _tile(seq, 256)
    blk_a = col_a // wc
    vec = lambda: pl.BlockSpec((1, wc), lambda b, j, i: (0, 0))
    return pl.pallas_call(
        functools.partial(_conformer_kernel, taps=taps),
        grid=(bsz, 1, seq // bm),
        in_specs=_halo_specs(bm, wc, seq, blk_a) + _halo_specs(bm, wc, seq, blk_a + 1)
        + [pl.BlockSpec((taps, wc), lambda b, j, i: (0, 0)), vec(), vec(), vec()],
        out_specs=pl.BlockSpec((1, bm, wc), lambda b, j, i: (b, i, 0)),
        out_shape=jax.ShapeDtypeStruct((bsz, seq, wc), BF16),
        scratch_shapes=[pltpu.VMEM((bm + 2 * BF16_TILE_ROWS, wc), F32), pltpu.VMEM((bm, wc), F32)],
        compiler_params=_params(("parallel", "parallel", "arbitrary"), 32),
        name="conformer_branch",
    )(z3, z3, z3, z3, z3, z3, w_dw, b_dw.reshape(1, wc), ln_g.reshape(1, wc), ln_b.reshape(1, wc))


CHUNKS_PER_STEP = 2
FWD, BWD = 0, 1


def _split3(x):
    hi = x.astype(BF16).astype(F32)
    rest = x - hi
    mid = rest.astype(BF16).astype(F32)
    lo = (rest - mid).astype(BF16).astype(F32)
    return jnp.concatenate([hi, mid, lo], axis=1)


def _column_form(sel3, pieces_row):
    rhs = jnp.broadcast_to(pieces_row, (LANES, pieces_row.shape[1])).astype(BF16)
    return lax.dot_general(sel3, rhs, (((1,), (1,)), ((), ())), preferred_element_type=F32)


def _mlstm_direction(q, k, v, a_row, lf3, a3, b_tot, m_prev, m_new, c_ref, n_ref, cum3, eye3, bias):
    reps = q.shape[1] // LANES
    wide = lambda x: jnp.concatenate([x] * reps, axis=1)
    b_col = _column_form(cum3, lf3)
    a_col = _column_form(eye3, a3)
    dmat = (b_col - a_row) + bias
    inter = b_col + m_prev
    m_t = jnp.maximum(inter, jnp.max(dmat, axis=-1, keepdims=True))
    w_intra = jnp.exp(dmat - m_t)
    w_inter = jnp.exp(inter - m_t)
    s = lax.dot_general(q, k, (((1,), (1,)), ((), ())), preferred_element_type=F32) * w_intra
    c_prev = c_ref[...]
    n_prev = n_ref[...]
    num = jnp.dot(s.astype(BF16), v, preferred_element_type=F32) + \
        wide(w_inter) * jnp.dot(q, c_prev.astype(BF16), preferred_element_type=F32)
    den = jnp.sum(s, axis=-1, keepdims=True) + \
        w_inter * jnp.sum(q.astype(F32) * n_prev, axis=-1, keepdims=True)
    h = num * wide(1.0 / jnp.maximum(jnp.abs(den), jnp.exp(-m_t)))
    w_s = jnp.exp(b_tot - m_new - a_col)
    decay = wide(jnp.exp(b_tot + m_prev - m_new))
    kw = k.astype(F32) * wide(w_s)
    c_ref[...] = decay * c_prev + lax.dot_general(kw.astype(BF16), v, (((0,), (0,)), ((), ())),
                                                  preferred_element_type=F32)
    n_ref[...] = decay * n_prev + jnp.sum(kw, axis=0, keepdims=True)
    return h


def _mlstm_kernel(q_ref, k_ref, v_ref, o_ref, grow_ref, g_ref, out_ref,
                  hs_ref, a_ref, lf3_ref, a3_ref, btot_ref, peak_ref, mprev_ref, mnew_ref,
                  sel_ref, bias_ref, c_ref, n_ref):
    length = CHUNK
    seq = q_ref.shape[1]
    n_chunks = seq // length
    row = lax.broadcasted_iota(I32, (length, length), 0)
    col = lax.broadcasted_iota(I32, (length, length), 1)
    lower = row >= col
    upper = row <= col
    lower_f = jnp.where(lower, 1.0, 0.0)
    upper_f = jnp.where(upper, 1.0, 0.0)
    eye_f = jnp.where(row == col, 1.0, 0.0)
    three = lambda m: jnp.concatenate([m, m, m], axis=1).astype(BF16)
    sel_ref[FWD] = three(lower_f)
    sel_ref[BWD] = three(upper_f)
    sel_ref[2] = three(eye_f)
    bias_ref[FWD] = jnp.where(lower, 0.0, NEG_INF)
    bias_ref[BWD] = jnp.where(upper, 0.0, NEG_INF)
    c_ref[...] = jnp.zeros(c_ref.shape, F32)
    n_ref[...] = jnp.zeros(n_ref.shape, F32)

    for d, cum_row, last in ((FWD, upper_f, length - 1), (BWD, lower_f, 0)):
        gate_i = grow_ref[0, 0, 2 * d]
        lf = _log_sigmoid(grow_ref[0, 0, 2 * d + 1])
        b_row = jnp.dot(lf, cum_row, preferred_element_type=F32, precision=HIGHEST)
        a = b_row - gate_i
        a_ref[d] = a
        lf3_ref[d] = _split3(lf)
        a3_ref[d] = _split3(a)
        btot_ref[d] = jnp.broadcast_to(b_row[:, last:last + 1], (n_chunks, LANES))
        peak_ref[d] = jnp.broadcast_to(jnp.max(-a, axis=-1, keepdims=True), (n_chunks, LANES))

    def stabiliser_scan(c, carry):
        new = []
        for d, m in zip((FWD, BWD), carry):
            i = c if d == FWD else n_chunks - 1 - c
            mprev_ref[d, pl.ds(i, 1), :] = m
            m = btot_ref[d, pl.ds(i, 1), :] + jnp.maximum(m, peak_ref[d, pl.ds(i, 1), :])
            mnew_ref[d, pl.ds(i, 1), :] = m
            new.append(m)
        return tuple(new)

    zero = jnp.zeros((1, LANES), F32)
    lax.fori_loop(0, n_chunks, stabiliser_scan, (zero, zero))

    def run(d, i):
        r0 = pl.multiple_of(i * length, length)
        one = pl.ds(i, 1)
        h = _mlstm_direction(q_ref[0, pl.ds(r0, length), :], k_ref[0, pl.ds(r0, length), :],
                             v_ref[0, pl.ds(r0, length), :], a_ref[d, one, :], lf3_ref[d, one, :],
                             a3_ref[d, one, :], btot_ref[d, one, :], mprev_ref[d, one, :], mnew_ref[d, one, :],
                             c_ref.at[d], n_ref.at[d], sel_ref[d], sel_ref[2], bias_ref[d])
        return r0, h

    def chunk_results(step):
        res = []
        for u in range(CHUNKS_PER_STEP):
            c = step * CHUNKS_PER_STEP + u
            res.append(run(FWD, c))
            res.append(run(BWD, n_chunks - 1 - c))
        return res

    def first_touch(step, carry):
        for r0, h in chunk_results(step):
            hs_ref[pl.ds(r0, length), :] = h
        return carry

    def second_touch(step, carry):
        for r0, h in chunk_results(step):
            hs = hs_ref[pl.ds(r0, length), :] + h
            gate = _sigmoid(o_ref[0, pl.ds(r0, length), :].astype(F32))
            out_ref[0, pl.ds(r0, length), :] = ((_rms(hs) * g_ref[...]) * gate).astype(out_ref.dtype)
        return carry

    steps = n_chunks // CHUNKS_PER_STEP
    lax.fori_loop(0, steps // 2, first_touch, 0)
    lax.fori_loop(steps // 2, steps, second_touch, 0)


def _mlstm_branch(qk3, z3, gates, m_norm_g, heads):
    bsz, seq, _ = z3.shape
    dh = HEAD_DIM
    wm = heads * dh
    n_chunks = seq // CHUNK
    assert CHUNK == LANES and dh % LANES == 0
    assert seq % CHUNK == 0 and n_chunks % (2 * CHUNKS_PER_STEP) == 0
    grow = gates.reshape(bsz, seq, 2, 2, heads).transpose(0, 4, 2, 3, 1).reshape(bsz, heads, 4, n_chunks, CHUNK)
    once = pl.Buffered(1)
    seq_blk = lambda col0: pl.BlockSpec((1, seq, dh), lambda b, h: (b, 0, col0 + h), pipeline_mode=once)
    per_chunk = lambda w: pltpu.VMEM((2, n_chunks, w), F32)
    return pl.pallas_call(
        _mlstm_kernel,
        grid=(bsz, heads),
        in_specs=[seq_blk(0), seq_blk(heads), seq_blk(2 * heads), seq_blk(3 * heads),
                  pl.BlockSpec((1, 1, 4, n_chunks, CHUNK), lambda b, h: (b, h, 0, 0, 0)),
                  pl.BlockSpec((1, dh), lambda b, h: (0, h))],
        out_specs=pl.BlockSpec((1, seq, dh), lambda b, h: (b, 0, h)),
        out_shape=jax.ShapeDtypeStruct((bsz, seq, wm), BF16),
        scratch_shapes=[pltpu.VMEM((seq, dh), F32),
                        per_chunk(CHUNK), per_chunk(3 * CHUNK), per_chunk(3 * CHUNK),
                        per_chunk(LANES), per_chunk(LANES), per_chunk(LANES), per_chunk(LANES),
                        pltpu.VMEM((3, CHUNK, 3 * CHUNK), BF16), pltpu.VMEM((2, CHUNK, CHUNK), F32),
                        pltpu.VMEM((2, dh, dh), F32), pltpu.VMEM((2, 1, dh), F32)],
        compiler_params=_params(("parallel", "parallel"), 52),
        name="mlstm",
    )(qk3, qk3, z3, z3, grow, m_norm_g.reshape(1, wm))


def _merge_kernel(hm_ref, uc_ref, gm_ref, gc_ref, x_ref, mod_ref, wm_ref, wc_ref, wo_ref, *rest, route):
    if route:
        rt_ref, xn_ref, hp_ref, lg_ref = rest
    else:
        xn_ref, hp_ref = rest
    y_m = jnp.dot(hm_ref[...], wm_ref[...], preferred_element_type=F32)
    y_c = jnp.dot(uc_ref[...], wc_ref[...], preferred_element_type=F32)
    merged = _sigmoid(gm_ref[...].astype(F32)) * y_m + _sigmoid(gc_ref[...].astype(F32)) * y_c
    out = jnp.dot(merged.astype(BF16), wo_ref[...], preferred_element_type=F32)
    x_new = x_ref[...] + mod_ref[0, 2:3, :] * out
    xn_ref[...] = x_new
    h = _rms(x_new) * (1.0 + mod_ref[0, 4:5, :]) + mod_ref[0, 3:4, :]
    hp_ref[...] = _pack_halves(h)
    if route:
        h_hi = h.astype(BF16)
        h_lo = (h - h_hi.astype(F32)).astype(BF16)
        lg_ref[...] = (jnp.dot(h_hi, rt_ref[0], preferred_element_type=F32)
                       + jnp.dot(h_lo, rt_ref[0], preferred_element_type=F32)
                       + jnp.dot(h_hi, rt_ref[1], preferred_element_type=F32))


def _merge(hm, uc, z2, col_gm, x2, mod_l, w_m, w_c, w_o, seq, router_t):
    t, d = x2.shape
    wm_, wc_ = hm.shape[1], uc.shape[1]
    bm = _tile(seq, 256)
    per_b = seq // bm
    assert col_gm % d == 0
    blk = col_gm // d
    route = router_t is not None
    once = pl.Buffered(1)
    in_specs = [pl.BlockSpec((bm, wm_), lambda i: (i, 0)),
                pl.BlockSpec((bm, wc_), lambda i: (i, 0)),
                pl.BlockSpec((bm, d), lambda i: (i, blk)),
                pl.BlockSpec((bm, d), lambda i: (i, blk + 1)),
                pl.BlockSpec((bm, d), lambda i: (i, 0)),
                pl.BlockSpec((1, 6, d), lambda i: (i // per_b, 0, 0)),
                pl.BlockSpec((wm_, d), lambda i: (0, 0), pipeline_mode=once),
                pl.BlockSpec((wc_, d), lambda i: (0, 0), pipeline_mode=once),
                pl.BlockSpec((d, d), lambda i: (0, 0), pipeline_mode=once)]
    out_specs = [pl.BlockSpec((bm, d), lambda i: (i, 0)), pl.BlockSpec((bm, d // 2), lambda i: (i, 0))]
    out_shape = [jax.ShapeDtypeStruct((t, d), F32), jax.ShapeDtypeStruct((t, d // 2), U32)]
    args = [hm, uc, z2, z2, x2, mod_l, w_m, w_c, w_o]
    if route:
        in_specs.append(pl.BlockSpec((2, d, LANES), lambda i: (0, 0, 0)))
        out_specs.append(pl.BlockSpec((bm, LANES), lambda i: (i, 0)))
        out_shape.append(jax.ShapeDtypeStruct((t, LANES), F32))
        args.append(router_t)
    return pl.pallas_call(
        functools.partial(_merge_kernel, route=route),
        grid=(t // bm,),
        in_specs=in_specs, out_specs=out_specs, out_shape=out_shape,
        compiler_params=_params(("parallel",), 48),
        name="merge",
    )(*args)


FFN_ROWS = 512
FIRST_OF_GROUP = 1
VALID = 2


def _ffn_up_kernel(te_ref, tf_ref, x_ref, w1_ref, w3_ref, o_ref, wb_ref):
    r = pl.program_id(1)
    flag = tf_ref[r]

    @pl.when((flag & FIRST_OF_GROUP) != 0)
    def _():
        wb_ref[0] = w1_ref[0].astype(BF16)
        wb_ref[1] = w3_ref[0].astype(BF16)

    @pl.when((flag & VALID) != 0)
    def _():
        lo, hi = _unpack_halves(x_ref[...])
        x = jnp.concatenate([lo.astype(BF16), hi.astype(BF16)], axis=1)
        a = jnp.dot(x, wb_ref[0], preferred_element_type=F32)
        g = jnp.dot(x, wb_ref[1], preferred_element_type=F32)
        o_ref[...] = (_silu(a) * g).astype(o_ref.dtype)

    @pl.when((flag & VALID) == 0)
    def _():
        o_ref[...] = jnp.zeros(o_ref.shape, o_ref.dtype)


def _ffn_up(xp, w13, tile_e, tile_f):
    rows, dh = xp.shape
    n_e, d, f2 = w13.shape
    f = f2 // 2
    bm = FFN_ROWS
    bn = _tile(f, 1024)
    nj = f // bn
    return pl.pallas_call(
        _ffn_up_kernel,
        grid_spec=pltpu.PrefetchScalarGridSpec(
            num_scalar_prefetch=2,
            grid=(nj, rows // bm),
            in_specs=[pl.BlockSpec((bm, dh), lambda j, r, te, tf: (r, 0)),
                      pl.BlockSpec((1, d, bn), lambda j, r, te, tf: (te[r], 0, j)),
                      pl.BlockSpec((1, d, bn), lambda j, r, te, tf: (te[r], 0, nj + j))],
            out_specs=pl.BlockSpec((bm, bn), lambda j, r, te, tf: (r, j)),
            scratch_shapes=[pltpu.VMEM((2, d, bn), BF16)]),
        out_shape=jax.ShapeDtypeStruct((rows, f), BF16),
        compiler_params=_params(("arbitrary", "arbitrary"), 52),
        name="ffn_up",
    )(tile_e, tile_f, xp, w13, w13)


def _ffn_down_kernel(te_ref, tf_ref, h_ref, wa_ref, wb_ref, o_ref):
    flag = tf_ref[pl.program_id(1)]

    @pl.when((flag & VALID) != 0)
    def _():
        h = h_ref[...]
        ya = jnp.dot(h, wa_ref[0], preferred_element_type=F32)
        yb = jnp.dot(h, wb_ref[0], preferred_element_type=F32)
        o_ref[...] = _pack_halves(jnp.concatenate([ya, yb], axis=1))

    @pl.when((flag & VALID) == 0)
    def _():
        o_ref[...] = jnp.zeros(o_ref.shape, o_ref.dtype)


def _ffn_down(h, w2, tile_e, tile_f):
    rows, f = h.shape
    n_e, _, d = w2.shape
    dh = d // 2
    bm = FFN_ROWS
    bn = _tile(dh, 256)
    nn = dh // bn
    return pl.pallas_call(
        _ffn_down_kernel,
        grid_spec=pltpu.PrefetchScalarGridSpec(
            num_scalar_prefetch=2,
            grid=(nn, rows // bm),
            in_specs=[pl.BlockSpec((bm, f), lambda n, r, te, tf: (r, 0)),
                      pl.BlockSpec((1, f, bn), lambda n, r, te, tf: (te[r], 0, n)),
                      pl.BlockSpec((1, f, bn), lambda n, r, te, tf: (te[r], 0, nn + n))],
            out_specs=pl.BlockSpec((bm, bn), lambda n, r, te, tf: (r, n))),
        out_shape=jax.ShapeDtypeStruct((rows, dh), U32),
        compiler_params=_params(("arbitrary", "arbitrary"), 48),
        name="ffn_down",
    )(tile_e, tile_f, h, w2, w2)


def _residual_epilogue(x, gate, f, mod_ref, fin_ref, xn_ref, hn_ref, final):
    x_new = x + gate * f
    if final:
        xn_ref[...] = _rms(x_new) * fin_ref[...]
    else:
        xn_ref[...] = x_new
        hn_ref[...] = (_rms(x_new) * (1.0 + mod_ref[0, 1:2, :]) + mod_ref[0, 0:1, :]).astype(hn_ref.dtype)


def _dense_residual_kernel(y_ref, x_ref, mod_ref, nmod_ref, fin_ref, xn_ref, *rest, final):
    hn_ref = None if final else rest[0]
    lo, hi = _unpack_halves(y_ref[...])
    f = jnp.concatenate([lo, hi], axis=1)
    _residual_epilogue(x_ref[...], mod_ref[0, 5:6, :], f, nmod_ref, fin_ref, xn_ref, hn_ref, final)


def _dense_residual(yp, x2, mod_l, mod_next, final_g, seq, final):
    t, d = x2.shape
    bm = _tile(seq, 512)
    per_b = seq // bm
    mod_spec = pl.BlockSpec((1, 6, d), lambda i: (i // per_b, 0, 0))
    row = lambda w: pl.BlockSpec((bm, w), lambda i: (i, 0))
    out_specs = [row(d)] + ([] if final else [row(d)])
    out_shape = [jax.ShapeDtypeStruct((t, d), F32)] + ([] if final else [jax.ShapeDtypeStruct((t, d), BF16)])
    return pl.pallas_call(
        functools.partial(_dense_residual_kernel, final=final),
        grid=(t // bm,),
        in_specs=[row(d // 2), row(d), mod_spec, mod_spec, pl.BlockSpec((1, d), lambda i: (0, 0))],
        out_specs=out_specs, out_shape=out_shape,
        compiler_params=_params(("parallel",), 40),
        name="dense_residual",
    )(yp, x2, mod_l, mod_next, final_g.reshape(1, d))


ROUTE_TOKENS = 512


def _route_kernel(lg_ref, idx_ref, wt_ref, cnt_ref, run_ref):
    step = pl.program_id(0)
    n_e, tb = lg_ref.shape

    @pl.when(step == 0)
    def _():
        run_ref[...] = jnp.zeros(run_ref.shape, F32)

    lg = lg_ref[...]
    eid = lax.broadcasted_iota(I32, (n_e, tb), 0)
    m1 = jnp.max(lg, axis=0, keepdims=True)
    e1 = jnp.min(jnp.where(lg == m1, eid, n_e), axis=0, keepdims=True)
    rest = jnp.where(eid == e1, NEG_INF, lg)
    m2 = jnp.max(rest, axis=0, keepdims=True)
    e2 = jnp.min(jnp.where(rest == m2, eid, n_e), axis=0, keepdims=True)
    p2 = jnp.exp(m2 - m1)
    w1 = 1.0 / (1.0 + p2)
    w2 = p2 / (1.0 + p2)
    member = jnp.where((eid == e1) | (eid == e2), 1.0, 0.0)
    before = (lax.broadcasted_iota(I32, (tb, tb), 0) < lax.broadcasted_iota(I32, (tb, tb), 1)).astype(BF16)
    rank = jnp.dot(member.astype(BF16), before, preferred_element_type=F32) + run_ref[:, 0:1]
    r1 = jnp.sum(jnp.where(eid == e1, rank, 0.0), axis=0, keepdims=True)
    r2 = jnp.sum(jnp.where(eid == e2, rank, 0.0), axis=0, keepdims=True)
    zero_i = jnp.zeros((SUBLANES - 4, tb), I32)
    idx_ref[...] = jnp.concatenate([e1, e2, r1.astype(I32), r2.astype(I32), zero_i], axis=0)
    wt_ref[...] = jnp.concatenate([w1, w2, jnp.zeros((SUBLANES - 2, tb), F32)], axis=0)
    run_ref[...] = run_ref[...] + jnp.sum(member, axis=1, keepdims=True)
    cnt_ref[...] = run_ref[...]


def _route(logits_t):
    n_e, t = logits_t.shape
    assert n_e % SUBLANES == 0
    tb = _tile(t, ROUTE_TOKENS)
    return pl.pallas_call(
        _route_kernel,
        grid=(t // tb,),
        in_specs=[pl.BlockSpec((n_e, tb), lambda i: (0, i))],
        out_specs=[pl.BlockSpec((SUBLANES, tb), lambda i: (0, i)),
                   pl.BlockSpec((SUBLANES, tb), lambda i: (0, i)),
                   pl.BlockSpec((n_e, LANES), lambda i: (0, 0))],
        out_shape=[jax.ShapeDtypeStruct((SUBLANES, t), I32), jax.ShapeDtypeStruct((SUBLANES, t), F32),
                   jax.ShapeDtypeStruct((n_e, LANES), F32)],
        scratch_shapes=[pltpu.VMEM((n_e, LANES), F32)],
        compiler_params=_params(("arbitrary",), 32),
        name="route",
    )(logits_t)


MOVE_TOKENS = 256
ISSUE_UNROLL = 8


def _row_copy(src, dst, sem):
    return pltpu.make_async_copy(src, dst, sem)


def _dispatch_kernel(dest_ref, x_ref, init_ref, out_ref, sem):
    del init_ref
    n_tok = x_ref.shape[0]
    total = dest_ref.shape[0] // TOP_K
    base = pl.program_id(0) * n_tok

    def issue(i, carry):
        for k in range(TOP_K):
            _row_copy(x_ref.at[pl.ds(i, 1)], out_ref.at[pl.ds(dest_ref[k * total + base + i], 1)],
                      sem).start(priority=k % 2)
        return carry

    lax.fori_loop(0, n_tok, issue, 0, unroll=ISSUE_UNROLL)
    for k in range(TOP_K):
        _row_copy(x_ref, out_ref.at[pl.ds(0, n_tok)], sem).wait()


def _dispatch(xp, dest_flat, rows):
    t, dh = xp.shape
    nt = _tile(t, MOVE_TOKENS)
    return pl.pallas_call(
        _dispatch_kernel,
        grid_spec=pltpu.PrefetchScalarGridSpec(
            num_scalar_prefetch=1,
            grid=(t // nt,),
            in_specs=[pl.BlockSpec((nt, dh), lambda i, dest: (i, 0)),
                      pl.BlockSpec(memory_space=pl.ANY)],
            out_specs=pl.BlockSpec(memory_space=pl.ANY),
            scratch_shapes=[pltpu.SemaphoreType.DMA(())]),
        out_shape=jax.ShapeDtypeStruct((rows, dh), U32),
        input_output_aliases={2: 0},
        compiler_params=_params(("arbitrary",), 32),
        name="dispatch",
    )(dest_flat, xp, jnp.zeros((rows, dh), U32))


def _combine_kernel(dest_ref, y_ref, wt_ref, x_ref, mod_ref, nmod_ref, fin_ref, xn_ref, *rest, final):
    if final:
        hn_ref = None
        buf_ref, sem = rest
    else:
        hn_ref, buf_ref, sem = rest
    n_tok = x_ref.shape[0]
    total = dest_ref.shape[0] // TOP_K
    step = pl.program_id(0)

    def gather(s):
        slot = s % 2

        def issue(i, carry):
            for k in range(TOP_K):
                _row_copy(y_ref.at[pl.ds(dest_ref[k * total + s * n_tok + i], 1)],
                          buf_ref.at[slot, k, pl.ds(i, 1)], sem.at[slot]).start(priority=k % 2)
            return carry

        lax.fori_loop(0, n_tok, issue, 0, unroll=ISSUE_UNROLL)

    @pl.when(step == 0)
    def _():
        gather(step)

    @pl.when(step + 1 < pl.num_programs(0))
    def _():
        gather(step + 1)

    slot = step % 2
    for k in range(TOP_K):
        _row_copy(y_ref.at[pl.ds(0, n_tok)], buf_ref.at[slot, k], sem.at[slot]).wait()
    f = None
    for k in range(TOP_K):
        lo, hi = _unpack_halves(buf_ref[slot, k])
        part = wt_ref[:, k:k + 1] * jnp.concatenate([lo, hi], axis=1)
        f = part if f is None else f + part
    _residual_epilogue(x_ref[...], mod_ref[0, 5:6, :], f, nmod_ref, fin_ref, xn_ref, hn_ref, final)


def _combine(yp, dest_flat, wt_cols, x2, mod_l, mod_next, final_g, seq, final):
    t, d = x2.shape
    dh = d // 2
    nt = _tile(seq, MOVE_TOKENS)
    per_b = seq // nt
    mod_spec = pl.BlockSpec((1, 6, d), lambda i, dest: (i // per_b, 0, 0))
    row = lambda w: pl.BlockSpec((nt, w), lambda i, dest: (i, 0))
    out_specs = [row(d)] + ([] if final else [row(d)])
    out_shape = [jax.ShapeDtypeStruct((t, d), F32)] + ([] if final else [jax.ShapeDtypeStruct((t, d), BF16)])
    return pl.pallas_call(
        functools.partial(_combine_kernel, final=final),
        grid_spec=pltpu.PrefetchScalarGridSpec(
            num_scalar_prefetch=1,
            grid=(t // nt,),
            in_specs=[pl.BlockSpec(memory_space=pl.ANY), row(TOP_K), row(d), mod_spec, mod_spec,
                      pl.BlockSpec((1, d), lambda i, dest: (0, 0))],
            out_specs=out_specs,
            scratch_shapes=[pltpu.VMEM((2, TOP_K, nt, dh), U32), pltpu.SemaphoreType.DMA((2,))]),
        out_shape=out_shape,
        compiler_params=_params(("arbitrary",), 40),
        name="combine",
    )(dest_flat, yp, wt_cols, x2, mod_l, mod_next, final_g.reshape(1, d))


def _dense_ffn(hp, x2, w13, w2, mod_l, mod_next, final_g, seq, final):
    t = hp.shape[0]
    assert t % FFN_ROWS == 0
    n_tiles = t // FFN_ROWS
    tile_e = jnp.zeros((n_tiles,), I32)
    tile_f = jnp.full((n_tiles,), VALID, I32).at[0].set(VALID | FIRST_OF_GROUP)
    hid = _ffn_up(hp, w13[None], tile_e, tile_f)
    yp = _ffn_down(hid, w2.astype(BF16)[None], tile_e, tile_f)
    return _dense_residual(yp, x2, mod_l, mod_next, final_g, seq, final)


def _moe_ffn(hp, logits_t, x2, w13, w2, mod_l, mod_next, final_g, seq, final):
    t = hp.shape[0]
    n_e = w13.shape[0]
    idx, wts, cnt = _route(logits_t)
    counts = cnt[:, 0].astype(I32)
    padded = ((counts + FFN_ROWS - 1) // FFN_ROWS) * FFN_ROWS
    seg_end = jnp.cumsum(padded)
    seg_start = seg_end - padded
    n_tiles = (t * TOP_K) // FFN_ROWS + n_e
    tile_row0 = jnp.arange(n_tiles, dtype=I32) * FFN_ROWS
    tile_e = jnp.minimum(jnp.searchsorted(seg_end, tile_row0, side="right"), n_e - 1).astype(I32)
    valid = tile_row0 < seg_end[-1]
    first = jnp.concatenate([jnp.ones((1,), bool), tile_e[1:] != tile_e[:-1]])
    tile_f = jnp.where(valid, VALID, 0) | jnp.where(first, FIRST_OF_GROUP, 0)
    start_of = jnp.sum(jnp.where(idx[:TOP_K, :, None] == jnp.arange(n_e, dtype=I32), seg_start, 0), axis=-1)
    dest = (start_of + idx[TOP_K:2 * TOP_K]).reshape(-1).astype(I32)
    xs = _dispatch(hp, dest, n_tiles * FFN_ROWS)
    hid = _ffn_up(xs, w13, tile_e, tile_f.astype(I32))
    yp = _ffn_down(hid, w2.astype(BF16), tile_e, tile_f.astype(I32))
    return _combine(yp, dest, wts[:TOP_K].T, x2, mod_l, mod_next, final_g, seq, final)


def kernel(x, c, w_mod, b_mod, w_in, b_in, w_qk_conv, m_norm_g, w_m_proj, w_dw, b_dw, ln_c_g, ln_c_b,
           w_c_proj, w_out, ffn_w13, ffn_w2, moe_router, moe_w13, moe_w2, final_g):
    bsz, seq, d = x.shape
    depth = w_mod.shape[0]
    wm = w_m_proj.shape[1]
    wc = w_dw.shape[2]
    heads = wm // HEAD_DIM
    n_gate = 4 * heads
    off_g = 4 * wm
    off_glu = off_g + n_gate
    t = bsz * seq
    assert d % (2 * LANES) == 0 and n_gate <= LANES

    mods = _adaln_mod(c, w_mod, b_mod)
    x2 = x.reshape(t, d)
    h = _prenorm(x2, mods[0], seq, 0, 1)
    out = None
    for l in range(depth):
        final = l == depth - 1
        mod_l = mods[l]
        mod_next = mods[l] if final else mods[l + 1]
        w_main = jnp.concatenate([w_in[l][:, :off_g], w_in[l][:, off_glu:]], axis=1).astype(BF16)
        b_main = jnp.concatenate([b_in[l][:off_g], b_in[l][off_glu:]])[None]
        w_gate = jnp.zeros((d, LANES), BF16).at[:, :n_gate].set(w_in[l][:, off_g:off_glu].astype(BF16))
        b_gate = jnp.zeros((1, LANES), F32).at[0, :n_gate].set(b_in[l][off_g:off_glu])
        z = _matmul_bias(h, w_main, b_main, BF16)
        gates = _matmul_bias(h, w_gate, b_gate, F32, bn_pref=LANES)[:, :n_gate]
        z3 = z.reshape(bsz, seq, z.shape[1])
        k_scale = jnp.concatenate([jnp.ones((wm,), F32), jnp.full((wm,), HEAD_DIM ** -0.5, F32)])
        qk3 = _qk_conv(z3, w_qk_conv[l] * k_scale[None], 2 * wm)
        hm = _mlstm_branch(qk3, z3, gates.reshape(bsz, seq, n_gate), m_norm_g[l], heads)
        uc = _conformer_branch(z3, off_g, w_dw[l], b_dw[l], ln_c_g[l], ln_c_b[l])
        moe = l % 2 == 1
        router_t = None
        if moe:
            n_e = moe_router.shape[2]
            r_pad = jnp.zeros((d, LANES), F32).at[:, :n_e].set(moe_router[l // 2])
            r_hi = r_pad.astype(BF16)
            router_t = jnp.stack([r_hi, (r_pad - r_hi.astype(F32)).astype(BF16)])
        res = _merge(hm.reshape(t, wm), uc.reshape(t, wc), z, off_g + 2 * wc, x2, mod_l,
                     w_m_proj[l].astype(BF16), w_c_proj[l].astype(BF16), w_out[l].astype(BF16), seq, router_t)
        if moe:
            x2, hp, logits = res
            res = _moe_ffn(hp, logits[:, :n_e].T, x2, moe_w13[l // 2], moe_w2[l // 2], mod_l, mod_next,
                           final_g, seq, final)
        else:
            x2, hp = res
            res = _dense_ffn(hp, x2, ffn_w13[l // 2], ffn_w2[l // 2], mod_l, mod_next, final_g, seq, final)
        if final:
            out = res[0]
        else:
            x2, h = res
    return out.reshape(bsz, seq, d)
```

```python
import functools
import math

import jax
import jax.numpy as jnp
from jax import lax
from jax.experimental import pallas as pl
from jax.experimental.pallas import tpu as pltpu

F32 = jnp.float32
BF16 = jnp.bfloat16
U32 = jnp.uint32
I32 = jnp.int32

HEAD_DIM = 256
CHUNK = 128
TOP_K = 2
EPS = 1e-6

V7X_VMEM_BYTES = 64 * 1024 * 1024
LANES = 128
SUBLANES = 8
BF16_TILE_ROWS = 16

HIGHEST = lax.Precision.HIGHEST
NEG_INF = float("-inf")


def _params(semantics, vmem_mib):
    assert vmem_mib * 1024 * 1024 < V7X_VMEM_BYTES
    return pltpu.CompilerParams(dimension_semantics=semantics, vmem_limit_bytes=vmem_mib * 1024 * 1024)


def _tile(n, pref):
    if n <= pref:
        return n
    t = pref - pref % LANES
    while t > LANES and n % t:
        t -= LANES
    assert n % t == 0, (n, pref)
    return t


def _sigmoid(x):
    return 1.0 / (1.0 + jnp.exp(-x))


def _silu(x):
    return x * _sigmoid(x)


def _log_sigmoid(x):
    return jnp.minimum(x, 0.0) - jnp.log1p(jnp.exp(-jnp.abs(x)))


def _rms(x):
    return x * lax.rsqrt(jnp.mean(x * x, axis=-1, keepdims=True) + EPS)


def _pack_halves(y):
    n = y.shape[-1] // 2
    lo = lax.bitcast_convert_type(y[:, :n].astype(BF16).astype(F32), U32)
    hi = lax.bitcast_convert_type(y[:, n:].astype(BF16).astype(F32), U32)
    return (hi & jnp.uint32(0xFFFF0000)) | (lo >> 16)


def _unpack_halves(u):
    lo = lax.bitcast_convert_type(u << 16, F32)
    hi = lax.bitcast_convert_type(u & jnp.uint32(0xFFFF0000), F32)
    return lo, hi


def _mod_kernel(c_ref, w_ref, b_ref, o_ref):
    c = c_ref[...]
    o_ref[0] = jnp.dot(_silu(c), w_ref[0], preferred_element_type=F32, precision=HIGHEST) + b_ref[0]


def _adaln_mod(c, w_mod, b_mod):
    depth, d, n = w_mod.shape
    b = c.shape[0]
    c_pad = jnp.zeros((SUBLANES, d), F32).at[:b].set(c)
    bn = _tile(n, 1024)
    out = pl.pallas_call(
        _mod_kernel,
        grid=(depth, n // bn),
        in_specs=[pl.BlockSpec((SUBLANES, d), lambda l, j: (0, 0)),
                  pl.BlockSpec((1, d, bn), lambda l, j: (l, 0, j)),
                  pl.BlockSpec((1, 1, bn), lambda l, j: (l, 0, j))],
        out_specs=pl.BlockSpec((1, SUBLANES, bn), lambda l, j: (l, 0, j)),
        out_shape=jax.ShapeDtypeStruct((depth, SUBLANES, n), F32),
        compiler_params=_params(("arbitrary", "arbitrary"), 40),
        name="adaln_mod",
    )(c_pad, w_mod, b_mod.reshape(depth, 1, n))
    return out[:, :b].reshape(depth, b, 6, d)


def _norm_kernel(x_ref, mod_ref, o_ref, *, shift, scale):
    x = x_ref[...]
    sh = mod_ref[0, shift:shift + 1, :]
    sc = mod_ref[0, scale:scale + 1, :]
    o_ref[...] = (_rms(x) * (1.0 + sc) + sh).astype(o_ref.dtype)


def _prenorm(x2, mod_l, seq, shift, scale):
    t, d = x2.shape
    bm = _tile(seq, 512)
    per_b = seq // bm
    return pl.pallas_call(
        functools.partial(_norm_kernel, shift=shift, scale=scale),
        grid=(t // bm,),
        in_specs=[pl.BlockSpec((bm, d), lambda i: (i, 0)),
                  pl.BlockSpec((1, 6, d), lambda i: (i // per_b, 0, 0))],
        out_specs=pl.BlockSpec((bm, d), lambda i: (i, 0)),
        out_shape=jax.ShapeDtypeStruct((t, d), BF16),
        compiler_params=_params(("parallel",), 32),
        name="prenorm",
    )(x2, mod_l)


def _mm_kernel(a_ref, w_ref, b_ref, o_ref):
    acc = jnp.dot(a_ref[...], w_ref[...], preferred_element_type=F32)
    o_ref[...] = (acc + b_ref[...]).astype(o_ref.dtype)


def _matmul_bias(a, w, bias, out_dtype, bm_pref=1024, bn_pref=1024):
    m, k = a.shape
    n = w.shape[1]
    bm = _tile(m, bm_pref)
    bn = _tile(n, bn_pref)
    return pl.pallas_call(
        _mm_kernel,
        grid=(n // bn, m // bm),
        in_specs=[pl.BlockSpec((bm, k), lambda j, i: (i, 0)),
                  pl.BlockSpec((k, bn), lambda j, i: (0, j)),
                  pl.BlockSpec((1, bn), lambda j, i: (0, j))],
        out_specs=pl.BlockSpec((bm, bn), lambda j, i: (i, j)),
        out_shape=jax.ShapeDtypeStruct((m, n), out_dtype),
        compiler_params=_params(("parallel", "arbitrary"), 48),
        name="matmul_bias",
    )(a, w, bias)


CONV_ROWS = 32
CONV_COLS = 256


def _fill_halo(xs_ref, prev, cur, nxt, first, last):
    hb = prev.shape[0]
    bm = cur.shape[0]
    xs_ref[0:hb, :] = jnp.where(first, 0.0, prev)
    xs_ref[hb:hb + bm, :] = cur
    xs_ref[hb + bm:hb + bm + hb, :] = jnp.where(last, 0.0, nxt)


def _conv_rows(xs_ref, w_ref, emit, *, taps, bm, width):
    pad = taps // 2
    lead = BF16_TILE_ROWS - pad
    span = CONV_ROWS + 2 * BF16_TILE_ROWS
    cb = min(CONV_COLS, width)
    assert bm % CONV_ROWS == 0 and width % cb == 0 and lead >= 0

    def body(rb, carry):
        r0 = pl.multiple_of(rb * CONV_ROWS, CONV_ROWS)
        for c0 in range(0, width, cb):
            blk = xs_ref[pl.ds(r0, span), c0:c0 + cb]
            acc = jnp.zeros((CONV_ROWS, cb), F32)
            for k in range(taps):
                acc = acc + w_ref[k:k + 1, c0:c0 + cb] * blk[lead + k:lead + k + CONV_ROWS, :]
            emit(r0, c0, acc)
        return carry

    lax.fori_loop(0, bm // CONV_ROWS, body, 0)


def _qkconv_kernel(prev_ref, cur_ref, next_ref, w_ref, o_ref, xs_ref, *, taps):
    i = pl.program_id(2)
    bm, width = cur_ref.shape[1], cur_ref.shape[2]
    _fill_halo(xs_ref, prev_ref[0].astype(F32), cur_ref[0].astype(F32), next_ref[0].astype(F32),
               i == 0, i == pl.num_programs(2) - 1)

    def emit(r0, c0, acc):
        o_ref[0, pl.ds(r0, CONV_ROWS), c0:c0 + acc.shape[1]] = acc.astype(o_ref.dtype)

    _conv_rows(xs_ref, w_ref, emit, taps=taps, bm=bm, width=width)


def _halo_specs(bm, cb, seq, col0):
    hb = BF16_TILE_ROWS
    per = bm // hb
    n_hb = seq // hb
    return [pl.BlockSpec((1, hb, cb), lambda b, j, i: (b, jnp.maximum(i * per - 1, 0), col0 + j)),
            pl.BlockSpec((1, bm, cb), lambda b, j, i: (b, i, col0 + j)),
            pl.BlockSpec((1, hb, cb), lambda b, j, i: (b, jnp.minimum((i + 1) * per, n_hb - 1), col0 + j))]


def _qk_conv(z3, w_qk, width):
    bsz, seq, _ = z3.shape
    taps = w_qk.shape[0]
    bm = _tile(seq, 512)
    cb = _tile(width, 2048)
    return pl.pallas_call(
        functools.partial(_qkconv_kernel, taps=taps),
        grid=(bsz, width // cb, seq // bm),
        in_specs=_halo_specs(bm, cb, seq, 0) + [pl.BlockSpec((taps, cb), lambda b, j, i: (0, j))],
        out_specs=pl.BlockSpec((1, bm, cb), lambda b, j, i: (b, i, j)),
        out_shape=jax.ShapeDtypeStruct((bsz, seq, width), BF16),
        scratch_shapes=[pltpu.VMEM((bm + 2 * BF16_TILE_ROWS, cb), F32)],
        compiler_params=_params(("parallel", "parallel", "arbitrary"), 32),
        name="qk_conv",
    )(z3, z3, z3, w_qk)


CONV_TOKENS = 8


def _glu_conv_kernel(prev_ref, cur_ref, next_ref, w_ref, b_ref, o_ref, u_ref, *, taps):
    i = pl.program_id(1)
    bm = cur_ref.shape[1]
    g = cur_ref.shape[2] // 2
    hb = BF16_TILE_ROWS
    lead = hb - taps // 2
    assert bm % CONV_TOKENS == 0 and lead >= 0

    def glu(blk):
        x = blk.astype(F32)
        return x[:, :g, :] * _sigmoid(x[:, g:, :])

    u_ref[0:hb] = jnp.where(i == 0, 0.0, glu(prev_ref[0]))
    u_ref[hb:hb + bm] = glu(cur_ref[0])
    u_ref[hb + bm:hb + bm + hb] = jnp.where(i == pl.num_programs(1) - 1, 0.0, glu(next_ref[0]))

    def body(grp, carry):
        t0 = grp * CONV_TOKENS
        acc = [b_ref[...]] * CONV_TOKENS
        for j in range(CONV_TOKENS + taps - 1):
            xin = u_ref[t0 + lead + j]
            for o in range(CONV_TOKENS):
                if 0 <= j - o < taps:
                    acc[o] = acc[o] + w_ref[j - o] * xin
        for o in range(CONV_TOKENS):
            o_ref[0, t0 + o] = acc[o]
        return carry

    lax.fori_loop(0, bm // CONV_TOKENS, body, 0)


def _glu_conv(glu4, w_dw, b_dw):
    bsz, seq, g2, _ = glu4.shape
    g = g2 // 2
    taps = w_dw.shape[0]
    bm = _tile(seq, 512)
    hb = BF16_TILE_ROWS
    per = bm // hb
    n_hb = seq // hb
    return pl.pallas_call(
        functools.partial(_glu_conv_kernel, taps=taps),
        grid=(bsz, seq // bm),
        in_specs=[pl.BlockSpec((1, hb, g2, LANES), lambda b, i: (b, jnp.maximum(i * per - 1, 0), 0, 0)),
                  pl.BlockSpec((1, bm, g2, LANES), lambda b, i: (b, i, 0, 0)),
                  pl.BlockSpec((1, hb, g2, LANES), lambda b, i: (b, jnp.minimum((i + 1) * per, n_hb - 1), 0, 0)),
                  pl.BlockSpec((taps, g, LANES), lambda b, i: (0, 0, 0)),
                  pl.BlockSpec((g, LANES), lambda b, i: (0, 0))],
        out_specs=pl.BlockSpec((1, bm, g, LANES), lambda b, i: (b, i, 0, 0)),
        out_shape=jax.ShapeDtypeStruct((bsz, seq, g, LANES), F32),
        scratch_shapes=[pltpu.VMEM((bm + 2 * hb, g, LANES), F32)],
        compiler_params=_params(("parallel", "arbitrary"), 32),
        name="glu_conv",
    )(glu4, glu4, glu4, w_dw.reshape(taps, g, LANES), b_dw.reshape(g, LANES))


CHUNKS_PER_STEP = 2
FWD, BWD = 0, 1


def _split3(x):
    hi = x.astype(BF16).astype(F32)
    rest = x - hi
    mid = rest.astype(BF16).astype(F32)
    lo = (rest - mid).astype(BF16).astype(F32)
    return jnp.concatenate([hi, mid, lo], axis=1)


def _column_form(sel3, pieces_row):
    rhs = jnp.broadcast_to(pieces_row, (LANES, pieces_row.shape[1])).astype(BF16)
    return lax.dot_general(sel3, rhs, (((1,), (1,)), ((), ())), preferred_element_type=F32)


def _mlstm_direction(q, k, v, a_row, lf3, a3, b_tot, m_prev, m_new, c_ref, n_ref, cum3, eye3, bias):
    reps = q.shape[1] // LANES
    wide = lambda x: jnp.concatenate([x] * reps, axis=1)
    b_col = _column_form(cum3, lf3)
    a_col = _column_form(eye3, a3)
    dmat = (b_col - a_row) + bias
    inter = b_col + m_prev
    m_t = jnp.maximum(inter, jnp.max(dmat, axis=-1, keepdims=True))
    w_intra = jnp.exp(dmat - m_t)
    w_inter = jnp.exp(inter - m_t)
    s = lax.dot_general(q, k, (((1,), (1,)), ((), ())), preferred_element_type=F32) * w_intra
    c_prev = c_ref[...]
    n_prev = n_ref[...]
    num = jnp.dot(s.astype(BF16), v, preferred_element_type=F32) + \
        wide(w_inter) * jnp.dot(q, c_prev.astype(BF16), preferred_element_type=F32)
    den = jnp.sum(s, axis=-1, keepdims=True) + \
        w_inter * jnp.sum(q.astype(F32) * n_prev, axis=-1, keepdims=True)
    h = num * wide(1.0 / jnp.maximum(jnp.abs(den), jnp.exp(-m_t)))
    w_s = jnp.exp(b_tot - m_new - a_col)
    decay = wide(jnp.exp(b_tot + m_prev - m_new))
    kw = k.astype(F32) * wide(w_s)
    c_ref[...] = decay * c_prev + lax.dot_general(kw.astype(BF16), v, (((0,), (0,)), ((), ())),
                                                  preferred_element_type=F32)
    n_ref[...] = decay * n_prev + jnp.sum(kw, axis=0, keepdims=True)
    return h


def _mlstm_kernel(q_ref, k_ref, v_ref, o_ref, grow_ref, g_ref, out_ref,
                  hs_ref, a_ref, lf3_ref, a3_ref, btot_ref, peak_ref, mprev_ref, mnew_ref,
                  sel_ref, bias_ref, c_ref, n_ref):
    length = CHUNK
    seq = q_ref.shape[1]
    n_chunks = seq // length
    row = lax.broadcasted_iota(I32, (length, length), 0)
    col = lax.broadcasted_iota(I32, (length, length), 1)
    lower = row >= col
    upper = row <= col
    lower_f = jnp.where(lower, 1.0, 0.0)
    upper_f = jnp.where(upper, 1.0, 0.0)
    eye_f = jnp.where(row == col, 1.0, 0.0)
    three = lambda m: jnp.concatenate([m, m, m], axis=1).astype(BF16)
    sel_ref[FWD] = three(lower_f)
    sel_ref[BWD] = three(upper_f)
    sel_ref[2] = three(eye_f)
    bias_ref[FWD] = jnp.where(lower, 0.0, NEG_INF)
    bias_ref[BWD] = jnp.where(upper, 0.0, NEG_INF)
    c_ref[...] = jnp.zeros(c_ref.shape, F32)
    n_ref[...] = jnp.zeros(n_ref.shape, F32)

    for d, cum_row, last in ((FWD, upper_f, length - 1), (BWD, lower_f, 0)):
        gate_i = grow_ref[0, 0, 2 * d]
        lf = _log_sigmoid(grow_ref[0, 0, 2 * d + 1])
        b_row = jnp.dot(lf, cum_row, preferred_element_type=F32, precision=HIGHEST)
        a = b_row - gate_i
        a_ref[d] = a
        lf3_ref[d] = _split3(lf)
        a3_ref[d] = _split3(a)
        btot_ref[d] = jnp.broadcast_to(b_row[:, last:last + 1], (n_chunks, LANES))
        peak_ref[d] = jnp.broadcast_to(jnp.max(-a, axis=-1, keepdims=True), (n_chunks, LANES))

    def stabiliser_scan(c, carry):
        new = []
        for d, m in zip((FWD, BWD), carry):
            i = c if d == FWD else n_chunks - 1 - c
            mprev_ref[d, pl.ds(i, 1), :] = m
            m = btot_ref[d, pl.ds(i, 1), :] + jnp.maximum(m, peak_ref[d, pl.ds(i, 1), :])
            mnew_ref[d, pl.ds(i, 1), :] = m
            new.append(m)
        return tuple(new)

    zero = jnp.zeros((1, LANES), F32)
    lax.fori_loop(0, n_chunks, stabiliser_scan, (zero, zero))

    def run(d, i):
        r0 = pl.multiple_of(i * length, length)
        one = pl.ds(i, 1)
        h = _mlstm_direction(q_ref[0, pl.ds(r0, length), :], k_ref[0, pl.ds(r0, length), :],
                             v_ref[0, pl.ds(r0, length), :], a_ref[d, one, :], lf3_ref[d, one, :],
                             a3_ref[d, one, :], btot_ref[d, one, :], mprev_ref[d, one, :], mnew_ref[d, one, :],
                             c_ref.at[d], n_ref.at[d], sel_ref[d], sel_ref[2], bias_ref[d])
        return r0, h

    def chunk_results(step):
        res = []
        for u in range(CHUNKS_PER_STEP):
            c = step * CHUNKS_PER_STEP + u
            res.append(run(FWD, c))
            res.append(run(BWD, n_chunks - 1 - c))
        return res

    def first_touch(step, carry):
        for r0, h in chunk_results(step):
            hs_ref[pl.ds(r0, length), :] = h
        return carry

    def second_touch(step, carry):
        for r0, h in chunk_results(step):
            hs = hs_ref[pl.ds(r0, length), :] + h
            gate = _sigmoid(o_ref[0, pl.ds(r0, length), :].astype(F32))
            out_ref[0, pl.ds(r0, length), :] = ((_rms(hs) * g_ref[...]) * gate).astype(out_ref.dtype)
        return carry

    steps = n_chunks // CHUNKS_PER_STEP
    lax.fori_loop(0, steps // 2, first_touch, 0)
    lax.fori_loop(steps // 2, steps, second_touch, 0)


def _mlstm_branch(qk3, z3, gates, m_norm_g, heads):
    bsz, seq, _ = z3.shape
    dh = HEAD_DIM
    wm = heads * dh
    n_chunks = seq // CHUNK
    assert CHUNK == LANES and dh % LANES == 0
    assert seq % CHUNK == 0 and n_chunks % (2 * CHUNKS_PER_STEP) == 0
    grow = gates.reshape(bsz, seq, 2, 2, heads).transpose(0, 4, 2, 3, 1).reshape(bsz, heads, 4, n_chunks, CHUNK)
    once = pl.Buffered(1)
    seq_blk = lambda col0: pl.BlockSpec((1, seq, dh), lambda b, h: (b, 0, col0 + h), pipeline_mode=once)
    per_chunk = lambda w: pltpu.VMEM((2, n_chunks, w), F32)
    return pl.pallas_call(
        _mlstm_kernel,
        grid=(bsz, heads),
        in_specs=[seq_blk(0), seq_blk(heads), seq_blk(2 * heads), seq_blk(3 * heads),
                  pl.BlockSpec((1, 1, 4, n_chunks, CHUNK), lambda b, h: (b, h, 0, 0, 0)),
                  pl.BlockSpec((1, dh), lambda b, h: (0, h))],
        out_specs=pl.BlockSpec((1, seq, dh), lambda b, h: (b, 0, h)),
        out_shape=jax.ShapeDtypeStruct((bsz, seq, wm), BF16),
        scratch_shapes=[pltpu.VMEM((seq, dh), F32),
                        per_chunk(CHUNK), per_chunk(3 * CHUNK), per_chunk(3 * CHUNK),
                        per_chunk(LANES), per_chunk(LANES), per_chunk(LANES), per_chunk(LANES),
                        pltpu.VMEM((3, CHUNK, 3 * CHUNK), BF16), pltpu.VMEM((2, CHUNK, CHUNK), F32),
                        pltpu.VMEM((2, dh, dh), F32), pltpu.VMEM((2, 1, dh), F32)],
        compiler_params=_params(("parallel", "parallel"), 52),
        name="mlstm",
    )(qk3, qk3, z3, z3, grow, m_norm_g.reshape(1, wm))


def _merge_kernel(hm_ref, uc_ref, ln_ref, gm_ref, gc_ref, x_ref, mod_ref, wm_ref, wc_ref, wo_ref, *rest, route):
    if route:
        rt_ref, xn_ref, hp_ref, lg_ref = rest
    else:
        xn_ref, hp_ref = rest
    y_m = jnp.dot(hm_ref[...], wm_ref[...], preferred_element_type=F32)
    yc = uc_ref[...]
    yc = yc - jnp.mean(yc, axis=-1, keepdims=True)
    var = jnp.mean(yc * yc, axis=-1, keepdims=True)
    u = _silu(yc * lax.rsqrt(var + EPS) * ln_ref[0:1, :] + ln_ref[1:2, :])
    y_c = jnp.dot(u.astype(BF16), wc_ref[...], preferred_element_type=F32)
    merged = _sigmoid(gm_ref[...].astype(F32)) * y_m + _sigmoid(gc_ref[...].astype(F32)) * y_c
    out = jnp.dot(merged.astype(BF16), wo_ref[...], preferred_element_type=F32)
    x_new = x_ref[...] + mod_ref[0, 2:3, :] * out
    xn_ref[...] = x_new
    h = _rms(x_new) * (1.0 + mod_ref[0, 4:5, :]) + mod_ref[0, 3:4, :]
    hp_ref[...] = _pack_halves(h)
    if route:
        h_hi = h.astype(BF16)
        h_lo = (h - h_hi.astype(F32)).astype(BF16)
        lg_ref[...] = (jnp.dot(h_hi, rt_ref[0], preferred_element_type=F32)
                       + jnp.dot(h_lo, rt_ref[0], preferred_element_type=F32)
                       + jnp.dot(h_hi, rt_ref[1], preferred_element_type=F32))


def _merge(hm, uc, ln_gb, z2, col_gm, x2, mod_l, w_m, w_c, w_o, seq, router_t):
    t, d = x2.shape
    wm_, wc_ = hm.shape[1], uc.shape[1]
    bm = _tile(seq, 256)
    per_b = seq // bm
    assert col_gm % d == 0
    blk = col_gm // d
    route = router_t is not None
    once = pl.Buffered(1)
    in_specs = [pl.BlockSpec((bm, wm_), lambda i: (i, 0)),
                pl.BlockSpec((bm, wc_), lambda i: (i, 0)),
                pl.BlockSpec((2, wc_), lambda i: (0, 0)),
                pl.BlockSpec((bm, d), lambda i: (i, blk)),
                pl.BlockSpec((bm, d), lambda i: (i, blk + 1)),
                pl.BlockSpec((bm, d), lambda i: (i, 0)),
                pl.BlockSpec((1, 6, d), lambda i: (i // per_b, 0, 0)),
                pl.BlockSpec((wm_, d), lambda i: (0, 0), pipeline_mode=once),
                pl.BlockSpec((wc_, d), lambda i: (0, 0), pipeline_mode=once),
                pl.BlockSpec((d, d), lambda i: (0, 0), pipeline_mode=once)]
    out_specs = [pl.BlockSpec((bm, d), lambda i: (i, 0)), pl.BlockSpec((bm, d // 2), lambda i: (i, 0))]
    out_shape = [jax.ShapeDtypeStruct((t, d), F32), jax.ShapeDtypeStruct((t, d // 2), U32)]
    args = [hm, uc, ln_gb, z2, z2, x2, mod_l, w_m, w_c, w_o]
    if route:
        in_specs.append(pl.BlockSpec((2, d, LANES), lambda i: (0, 0, 0)))
        out_specs.append(pl.BlockSpec((bm, LANES), lambda i: (i, 0)))
        out_shape.append(jax.ShapeDtypeStruct((t, LANES), F32))
        args.append(router_t)
    return pl.pallas_call(
        functools.partial(_merge_kernel, route=route),
        grid=(t // bm,),
        in_specs=in_specs, out_specs=out_specs, out_shape=out_shape,
        compiler_params=_params(("parallel",), 48),
        name="merge",
    )(*args)


FFN_ROWS = 512
FIRST_OF_GROUP = 1
VALID = 2


def _ffn_up_kernel(te_ref, tf_ref, x_ref, w1_ref, w3_ref, o_ref, wb_ref):
    r = pl.program_id(1)
    flag = tf_ref[r]

    @pl.when((flag & FIRST_OF_GROUP) != 0)
    def _():
        wb_ref[0] = w1_ref[0].astype(BF16)
        wb_ref[1] = w3_ref[0].astype(BF16)

    @pl.when((flag & VALID) != 0)
    def _():
        lo, hi = _unpack_halves(x_ref[...])
        x = jnp.concatenate([lo.astype(BF16), hi.astype(BF16)], axis=1)
        a = jnp.dot(x, wb_ref[0], preferred_element_type=F32)
        g = jnp.dot(x, wb_ref[1], preferred_element_type=F32)
        o_ref[...] = (_silu(a) * g).astype(o_ref.dtype)

    @pl.when((flag & VALID) == 0)
    def _():
        o_ref[...] = jnp.zeros(o_ref.shape, o_ref.dtype)


def _ffn_up(xp, w13, tile_e, tile_f):
    rows, dh = xp.shape
    n_e, d, f2 = w13.shape
    f = f2 // 2
    bm = FFN_ROWS
    bn = _tile(f, 1024)
    nj = f // bn
    return pl.pallas_call(
        _ffn_up_kernel,
        grid_spec=pltpu.PrefetchScalarGridSpec(
            num_scalar_prefetch=2,
            grid=(nj, rows // bm),
            in_specs=[pl.BlockSpec((bm, dh), lambda j, r, te, tf: (r, 0)),
                      pl.BlockSpec((1, d, bn), lambda j, r, te, tf: (te[r], 0, j)),
                      pl.BlockSpec((1, d, bn), lambda j, r, te, tf: (te[r], 0, nj + j))],
            out_specs=pl.BlockSpec((bm, bn), lambda j, r, te, tf: (r, j)),
            scratch_shapes=[pltpu.VMEM((2, d, bn), BF16)]),
        out_shape=jax.ShapeDtypeStruct((rows, f), BF16),
        compiler_params=_params(("arbitrary", "arbitrary"), 52),
        name="ffn_up",
    )(tile_e, tile_f, xp, w13, w13)


CAST_ROWS = 512


def _ffn_down_kernel(te_ref, tf_ref, tn_ref, h_ref, w_hbm, o_ref, stage_ref, wb_ref, sem):
    n = pl.program_id(0)
    r = pl.program_id(1)
    flag = tf_ref[r]
    f, bn = stage_ref.shape[1], stage_ref.shape[2]
    half = w_hbm.shape[2] // 2

    def fetch(e, col_block):
        c0 = col_block * bn
        return [pltpu.make_async_copy(w_hbm.at[e, :, pl.ds(pl.multiple_of(c0 + k * half, LANES), bn)],
                                      stage_ref.at[k], sem.at[k]) for k in range(2)]

    @pl.when((flag & FIRST_OF_GROUP) != 0)
    def _():
        @pl.when((n == 0) & (r == 0))
        def _():
            for cp in fetch(te_ref[0], 0):
                cp.start()

        for cp in fetch(te_ref[r], n):
            cp.wait()

        step_rows = math.gcd(f, CAST_ROWS)

        def cast(i, carry):
            rows = pl.ds(pl.multiple_of(i * step_rows, step_rows), step_rows)
            wb_ref[:, rows, :] = stage_ref[:, rows, :].astype(BF16)
            return carry

        lax.fori_loop(0, f // step_rows, cast, 0)
        nxt = tn_ref[r]

        @pl.when(nxt >= 0)
        def _():
            for cp in fetch(nxt, n):
                cp.start()

        @pl.when((nxt < 0) & (n + 1 < pl.num_programs(0)))
        def _():
            for cp in fetch(te_ref[0], n + 1):
                cp.start()

    @pl.when((flag & VALID) != 0)
    def _():
        h = h_ref[...]
        ya = jnp.dot(h, wb_ref[0], preferred_element_type=F32)
        yb = jnp.dot(h, wb_ref[1], preferred_element_type=F32)
        o_ref[...] = _pack_halves(jnp.concatenate([ya, yb], axis=1))

    @pl.when((flag & VALID) == 0)
    def _():
        o_ref[...] = jnp.zeros(o_ref.shape, o_ref.dtype)


def _ffn_down(h, w2, tile_e, tile_f, tile_next):
    rows, f = h.shape
    n_e, _, d = w2.shape
    dh = d // 2
    bm = FFN_ROWS
    bn = _tile(dh, 256)
    nn = dh // bn
    assert math.gcd(f, CAST_ROWS) % BF16_TILE_ROWS == 0
    return pl.pallas_call(
        _ffn_down_kernel,
        grid_spec=pltpu.PrefetchScalarGridSpec(
            num_scalar_prefetch=3,
            grid=(nn, rows // bm),
            in_specs=[pl.BlockSpec((bm, f), lambda n, r, te, tf, tn: (r, 0)),
                      pl.BlockSpec(memory_space=pl.ANY)],
            out_specs=pl.BlockSpec((bm, bn), lambda n, r, te, tf, tn: (r, n)),
            scratch_shapes=[pltpu.VMEM((2, f, bn), F32), pltpu.VMEM((2, f, bn), BF16),
                            pltpu.SemaphoreType.DMA((2,))]),
        out_shape=jax.ShapeDtypeStruct((rows, dh), U32),
        compiler_params=_params(("arbitrary", "arbitrary"), 48),
        name="ffn_down",
    )(tile_e, tile_f, tile_next, h, w2)


def _residual_epilogue(x, gate, f, mod_ref, fin_ref, xn_ref, hn_ref, final):
    x_new = x + gate * f
    if final:
        xn_ref[...] = _rms(x_new) * fin_ref[...]
    else:
        xn_ref[...] = x_new
        hn_ref[...] = (_rms(x_new) * (1.0 + mod_ref[0, 1:2, :]) + mod_ref[0, 0:1, :]).astype(hn_ref.dtype)


def _dense_residual_kernel(y_ref, x_ref, mod_ref, nmod_ref, fin_ref, xn_ref, *rest, final):
    hn_ref = None if final else rest[0]
    lo, hi = _unpack_halves(y_ref[...])
    f = jnp.concatenate([lo, hi], axis=1)
    _residual_epilogue(x_ref[...], mod_ref[0, 5:6, :], f, nmod_ref, fin_ref, xn_ref, hn_ref, final)


def _dense_residual(yp, x2, mod_l, mod_next, final_g, seq, final):
    t, d = x2.shape
    bm = _tile(seq, 512)
    per_b = seq // bm
    mod_spec = pl.BlockSpec((1, 6, d), lambda i: (i // per_b, 0, 0))
    row = lambda w: pl.BlockSpec((bm, w), lambda i: (i, 0))
    out_specs = [row(d)] + ([] if final else [row(d)])
    out_shape = [jax.ShapeDtypeStruct((t, d), F32)] + ([] if final else [jax.ShapeDtypeStruct((t, d), BF16)])
    return pl.pallas_call(
        functools.partial(_dense_residual_kernel, final=final),
        grid=(t // bm,),
        in_specs=[row(d // 2), row(d), mod_spec, mod_spec, pl.BlockSpec((1, d), lambda i: (0, 0))],
        out_specs=out_specs, out_shape=out_shape,
        compiler_params=_params(("parallel",), 40),
        name="dense_residual",
    )(yp, x2, mod_l, mod_next, final_g.reshape(1, d))


ROUTE_TOKENS = 512


def _route_kernel(lg_ref, idx_ref, wt_ref, cnt_ref, run_ref):
    step = pl.program_id(0)
    n_e, tb = lg_ref.shape

    @pl.when(step == 0)
    def _():
        run_ref[...] = jnp.zeros(run_ref.shape, F32)

    lg = lg_ref[...]
    eid = lax.broadcasted_iota(I32, (n_e, tb), 0)
    m1 = jnp.max(lg, axis=0, keepdims=True)
    e1 = jnp.min(jnp.where(lg == m1, eid, n_e), axis=0, keepdims=True)
    rest = jnp.where(eid == e1, NEG_INF, lg)
    m2 = jnp.max(rest, axis=0, keepdims=True)
    e2 = jnp.min(jnp.where(rest == m2, eid, n_e), axis=0, keepdims=True)
    p2 = jnp.exp(m2 - m1)
    w1 = 1.0 / (1.0 + p2)
    w2 = p2 / (1.0 + p2)
    member = jnp.where((eid == e1) | (eid == e2), 1.0, 0.0)
    before = (lax.broadcasted_iota(I32, (tb, tb), 0) < lax.broadcasted_iota(I32, (tb, tb), 1)).astype(BF16)
    rank = jnp.dot(member.astype(BF16), before, preferred_element_type=F32) + run_ref[:, 0:1]
    r1 = jnp.sum(jnp.where(eid == e1, rank, 0.0), axis=0, keepdims=True)
    r2 = jnp.sum(jnp.where(eid == e2, rank, 0.0), axis=0, keepdims=True)
    zero_i = jnp.zeros((SUBLANES - 4, tb), I32)
    idx_ref[...] = jnp.concatenate([e1, e2, r1.astype(I32), r2.astype(I32), zero_i], axis=0)
    wt_ref[...] = jnp.concatenate([w1, w2, jnp.zeros((SUBLANES - 2, tb), F32)], axis=0)
    run_ref[...] = run_ref[...] + jnp.sum(member, axis=1, keepdims=True)
    cnt_ref[...] = run_ref[...]


def _route(logits_t):
    n_e, t = logits_t.shape
    assert n_e % SUBLANES == 0
    tb = _tile(t, ROUTE_TOKENS)
    return pl.pallas_call(
        _route_kernel,
        grid=(t // tb,),
        in_specs=[pl.BlockSpec((n_e, tb), lambda i: (0, i))],
        out_specs=[pl.BlockSpec((SUBLANES, tb), lambda i: (0, i)),
                   pl.BlockSpec((SUBLANES, tb), lambda i: (0, i)),
                   pl.BlockSpec((n_e, LANES), lambda i: (0, 0))],
        out_shape=[jax.ShapeDtypeStruct((SUBLANES, t), I32), jax.ShapeDtypeStruct((SUBLANES, t), F32),
                   jax.ShapeDtypeStruct((n_e, LANES), F32)],
        scratch_shapes=[pltpu.VMEM((n_e, LANES), F32)],
        compiler_params=_params(("arbitrary",), 32),
        name="route",
    )(logits_t)


MOVE_TOKENS = 256
ISSUE_UNROLL = 8


def _row_copy(src, dst, sem):
    return pltpu.make_async_copy(src, dst, sem)


def _dispatch_kernel(dest_ref, x_ref, init_ref, out_ref, sem):
    del init_ref
    n_tok = x_ref.shape[0]
    total = dest_ref.shape[0] // TOP_K
    base = pl.program_id(0) * n_tok

    def issue(i, carry):
        for k in range(TOP_K):
            _row_copy(x_ref.at[pl.ds(i, 1)], out_ref.at[pl.ds(dest_ref[k * total + base + i], 1)],
                      sem).start(priority=k % 2)
        return carry

    lax.fori_loop(0, n_tok, issue, 0, unroll=ISSUE_UNROLL)
    for k in range(TOP_K):
        _row_copy(x_ref, out_ref.at[pl.ds(0, n_tok)], sem).wait()


def _dispatch(xp, dest_flat, rows):
    t, dh = xp.shape
    nt = _tile(t, MOVE_TOKENS)
    return pl.pallas_call(
        _dispatch_kernel,
        grid_spec=pltpu.PrefetchScalarGridSpec(
            num_scalar_prefetch=1,
            grid=(t // nt,),
            in_specs=[pl.BlockSpec((nt, dh), lambda i, dest: (i, 0)),
                      pl.BlockSpec(memory_space=pl.ANY)],
            out_specs=pl.BlockSpec(memory_space=pl.ANY),
            scratch_shapes=[pltpu.SemaphoreType.DMA(())]),
        out_shape=jax.ShapeDtypeStruct((rows, dh), U32),
        input_output_aliases={2: 0},
        compiler_params=_params(("arbitrary",), 32),
        name="dispatch",
    )(dest_flat, xp, jnp.zeros((rows, dh), U32))


def _combine_kernel(dest_ref, y_ref, wt_ref, x_ref, mod_ref, nmod_ref, fin_ref, xn_ref, *rest, final):
    if final:
        hn_ref = None
        buf_ref, sem = rest
    else:
        hn_ref, buf_ref, sem = rest
    n_tok = x_ref.shape[0]
    total = dest_ref.shape[0] // TOP_K
    step = pl.program_id(0)

    def gather(s):
        slot = s % 2

        def issue(i, carry):
            for k in range(TOP_K):
                _row_copy(y_ref.at[pl.ds(dest_ref[k * total + s * n_tok + i], 1)],
                          buf_ref.at[slot, k, pl.ds(i, 1)], sem.at[slot]).start(priority=k % 2)
            return carry

        lax.fori_loop(0, n_tok, issue, 0, unroll=ISSUE_UNROLL)

    @pl.when(step == 0)
    def _():
        gather(step)

    @pl.when(step + 1 < pl.num_programs(0))
    def _():
        gather(step + 1)

    slot = step % 2
    for k in range(TOP_K):
        _row_copy(y_ref.at[pl.ds(0, n_tok)], buf_ref.at[slot, k], sem.at[slot]).wait()
    f = None
    for k in range(TOP_K):
        lo, hi = _unpack_halves(buf_ref[slot, k])
        part = wt_ref[:, k:k + 1] * jnp.concatenate([lo, hi], axis=1)
        f = part if f is None else f + part
    _residual_epilogue(x_ref[...], mod_ref[0, 5:6, :], f, nmod_ref, fin_ref, xn_ref, hn_ref, final)


def _combine(yp, dest_flat, wt_cols, x2, mod_l, mod_next, final_g, seq, final):
    t, d = x2.shape
    dh = d // 2
    nt = _tile(seq, MOVE_TOKENS)
    per_b = seq // nt
    mod_spec = pl.BlockSpec((1, 6, d), lambda i, dest: (i // per_b, 0, 0))
    row = lambda w: pl.BlockSpec((nt, w), lambda i, dest: (i, 0))
    out_specs = [row(d)] + ([] if final else [row(d)])
    out_shape = [jax.ShapeDtypeStruct((t, d), F32)] + ([] if final else [jax.ShapeDtypeStruct((t, d), BF16)])
    return pl.pallas_call(
        functools.partial(_combine_kernel, final=final),
        grid_spec=pltpu.PrefetchScalarGridSpec(
            num_scalar_prefetch=1,
            grid=(t // nt,),
            in_specs=[pl.BlockSpec(memory_space=pl.ANY), row(TOP_K), row(d), mod_spec, mod_spec,
                      pl.BlockSpec((1, d), lambda i, dest: (0, 0))],
            out_specs=out_specs,
            scratch_shapes=[pltpu.VMEM((2, TOP_K, nt, dh), U32), pltpu.SemaphoreType.DMA((2,))]),
        out_shape=out_shape,
        compiler_params=_params(("arbitrary",), 40),
        name="combine",
    )(dest_flat, yp, wt_cols, x2, mod_l, mod_next, final_g.reshape(1, d))


def _dense_ffn(hp, x2, w13, w2, mod_l, mod_next, final_g, seq, final):
    t = hp.shape[0]
    assert t % FFN_ROWS == 0
    n_tiles = t // FFN_ROWS
    tile_e = jnp.zeros((n_tiles,), I32)
    tile_f = jnp.full((n_tiles,), VALID, I32).at[0].set(VALID | FIRST_OF_GROUP)
    hid = _ffn_up(hp, w13[None], tile_e, tile_f)
    yp = _ffn_down(hid, w2[None], tile_e, tile_f, jnp.full((n_tiles,), -1, I32))
    return _dense_residual(yp, x2, mod_l, mod_next, final_g, seq, final)


def _moe_ffn(hp, logits_t, x2, w13, w2, mod_l, mod_next, final_g, seq, final):
    t = hp.shape[0]
    n_e = w13.shape[0]
    idx, wts, cnt = _route(logits_t)
    counts = cnt[:, 0].astype(I32)
    padded = ((counts + FFN_ROWS - 1) // FFN_ROWS) * FFN_ROWS
    seg_end = jnp.cumsum(padded)
    seg_start = seg_end - padded
    n_tiles = (t * TOP_K) // FFN_ROWS + n_e
    tile_row0 = jnp.arange(n_tiles, dtype=I32) * FFN_ROWS
    tile_e = jnp.minimum(jnp.searchsorted(seg_end, tile_row0, side="right"), n_e - 1).astype(I32)
    valid = tile_row0 < seg_end[-1]
    first = jnp.concatenate([jnp.ones((1,), bool), tile_e[1:] != tile_e[:-1]]) & valid
    tile_f = (jnp.where(valid, VALID, 0) | jnp.where(first, FIRST_OF_GROUP, 0)).astype(I32)
    tile_id = jnp.arange(n_tiles, dtype=I32)
    later_start = lax.cummin(jnp.where(first, tile_id, n_tiles), reverse=True)
    next_start = jnp.concatenate([later_start[1:], jnp.full((1,), n_tiles, I32)])
    tile_next = jnp.where(next_start < n_tiles, tile_e[jnp.minimum(next_start, n_tiles - 1)], -1).astype(I32)
    start_of = jnp.sum(jnp.where(idx[:TOP_K, :, None] == jnp.arange(n_e, dtype=I32), seg_start, 0), axis=-1)
    dest = (start_of + idx[TOP_K:2 * TOP_K]).reshape(-1).astype(I32)
    xs = _dispatch(hp, dest, n_tiles * FFN_ROWS)
    hid = _ffn_up(xs, w13, tile_e, tile_f)
    yp = _ffn_down(hid, w2, tile_e, tile_f, tile_next)
    return _combine(yp, dest, wts[:TOP_K].T, x2, mod_l, mod_next, final_g, seq, final)


def kernel(x, c, w_mod, b_mod, w_in, b_in, w_qk_conv, m_norm_g, w_m_proj, w_dw, b_dw, ln_c_g, ln_c_b,
           w_c_proj, w_out, ffn_w13, ffn_w2, moe_router, moe_w13, moe_w2, final_g):
    bsz, seq, d = x.shape
    depth = w_mod.shape[0]
    wm = w_m_proj.shape[1]
    wc = w_dw.shape[2]
    heads = wm // HEAD_DIM
    n_gate = 4 * heads
    off_g = 4 * wm
    off_glu = off_g + n_gate
    t = bsz * seq
    assert d % (2 * LANES) == 0 and n_gate <= LANES

    mods = _adaln_mod(c, w_mod, b_mod)
    x2 = x.reshape(t, d)
    h = _prenorm(x2, mods[0], seq, 0, 1)
    out = None
    for l in range(depth):
        final = l == depth - 1
        mod_l = mods[l]
        mod_next = mods[l] if final else mods[l + 1]
        w_main = jnp.concatenate([w_in[l][:, :off_g], w_in[l][:, off_glu:]], axis=1).astype(BF16)
        b_main = jnp.concatenate([b_in[l][:off_g], b_in[l][off_glu:]])[None]
        w_gate = jnp.zeros((d, LANES), BF16).at[:, :n_gate].set(w_in[l][:, off_g:off_glu].astype(BF16))
        b_gate = jnp.zeros((1, LANES), F32).at[0, :n_gate].set(b_in[l][off_g:off_glu])
        z = _matmul_bias(h, w_main, b_main, BF16)
        gates = _matmul_bias(h, w_gate, b_gate, F32, bn_pref=LANES)[:, :n_gate]
        z3 = z.reshape(bsz, seq, z.shape[1])
        k_scale = jnp.concatenate([jnp.ones((wm,), F32), jnp.full((wm,), HEAD_DIM ** -0.5, F32)])
        qk3 = _qk_conv(z3, w_qk_conv[l] * k_scale[None], 2 * wm)
        hm = _mlstm_branch(qk3, z3, gates.reshape(bsz, seq, n_gate), m_norm_g[l], heads)
        glu4 = z[:, off_g:off_g + 2 * wc].reshape(bsz, seq, 2 * wc // LANES, LANES)
        uc = _glu_conv(glu4, w_dw[l], b_dw[l])
        ln_gb = jnp.stack([ln_c_g[l], ln_c_b[l]])
        moe = l % 2 == 1
        router_t = None
        if moe:
            n_e = moe_router.shape[2]
            r_pad = jnp.zeros((d, LANES), F32).at[:, :n_e].set(moe_router[l // 2])
            r_hi = r_pad.astype(BF16)
            router_t = jnp.stack([r_hi, (r_pad - r_hi.astype(F32)).astype(BF16)])
        res = _merge(hm.reshape(t, wm), uc.reshape(t, wc), ln_gb, z, off_g + 2 * wc, x2, mod_l,
                     w_m_proj[l].astype(BF16), w_c_proj[l].astype(BF16), w_out[l].astype(BF16), seq, router_t)
        if moe:
            x2, hp, logits = res
            res = _moe_ffn(hp, logits[:, :n_e].T, x2, moe_w13[l // 2], moe_w2[l // 2], mod_l, mod_next,
                           final_g, seq, final)
        else:
            x2, hp = res
            res = _dense_ffn(hp, x2, ffn_w13[l // 2], ffn_w2[l // 2], mod_l, mod_next, final_g, seq, final)
        if final:
            out = res[0]
        else:
            x2, h = res
    return out.reshape(bsz, seq, d)
```

```python
import functools
import math

import jax
import jax.numpy as jnp
from jax import lax
from jax.experimental import pallas as pl
from jax.experimental.pallas import tpu as pltpu

F32 = jnp.float32
BF16 = jnp.bfloat16
U32 = jnp.uint32
I32 = jnp.int32

HEAD_DIM = 256
CHUNK = 128
TOP_K = 2
EPS = 1e-6

V7X_VMEM_BYTES = 64 * 1024 * 1024
LANES = 128
SUBLANES = 8
BF16_TILE_ROWS = 16

HIGHEST = lax.Precision.HIGHEST
NEG_INF = float("-inf")


def _params(semantics, vmem_mib):
    assert vmem_mib * 1024 * 1024 < V7X_VMEM_BYTES
    return pltpu.CompilerParams(dimension_semantics=semantics, vmem_limit_bytes=vmem_mib * 1024 * 1024)


def _tile(n, pref):
    if n <= pref:
        return n
    t = pref - pref % LANES
    while t > LANES and n % t:
        t -= LANES
    assert n % t == 0, (n, pref)
    return t


def _sigmoid(x):
    return 1.0 / (1.0 + jnp.exp(-x))


def _silu(x):
    return x * _sigmoid(x)


def _log_sigmoid(x):
    return jnp.minimum(x, 0.0) - jnp.log1p(jnp.exp(-jnp.abs(x)))


def _rms(x):
    return x * lax.rsqrt(jnp.mean(x * x, axis=-1, keepdims=True) + EPS)


def _pack_halves(y):
    n = y.shape[-1] // 2
    lo = lax.bitcast_convert_type(y[:, :n].astype(BF16).astype(F32), U32)
    hi = lax.bitcast_convert_type(y[:, n:].astype(BF16).astype(F32), U32)
    return (hi & jnp.uint32(0xFFFF0000)) | (lo >> 16)


def _unpack_halves(u):
    lo = lax.bitcast_convert_type(u << 16, F32)
    hi = lax.bitcast_convert_type(u & jnp.uint32(0xFFFF0000), F32)
    return lo, hi


def _mod_kernel(c_ref, w_ref, b_ref, o_ref):
    c = _silu(c_ref[...])
    w = w_ref[0]
    c_hi = c.astype(BF16)
    c_lo = (c - c_hi.astype(F32)).astype(BF16)
    w_hi = w.astype(BF16)
    w_lo = (w - w_hi.astype(F32)).astype(BF16)
    o_ref[0] = (jnp.dot(c_hi, w_hi, preferred_element_type=F32) + jnp.dot(c_lo, w_hi, preferred_element_type=F32)
                + jnp.dot(c_hi, w_lo, preferred_element_type=F32) + b_ref[0])


def _adaln_mod(c, w_mod, b_mod):
    depth, d, n = w_mod.shape
    b = c.shape[0]
    c_pad = jnp.zeros((SUBLANES, d), F32).at[:b].set(c)
    bn = _tile(n, 1024)
    out = pl.pallas_call(
        _mod_kernel,
        grid=(depth, n // bn),
        in_specs=[pl.BlockSpec((SUBLANES, d), lambda l, j: (0, 0)),
                  pl.BlockSpec((1, d, bn), lambda l, j: (l, 0, j)),
                  pl.BlockSpec((1, 1, bn), lambda l, j: (l, 0, j))],
        out_specs=pl.BlockSpec((1, SUBLANES, bn), lambda l, j: (l, 0, j)),
        out_shape=jax.ShapeDtypeStruct((depth, SUBLANES, n), F32),
        compiler_params=_params(("arbitrary", "arbitrary"), 40),
        name="adaln_mod",
    )(c_pad, w_mod, b_mod.reshape(depth, 1, n))
    return out[:, :b].reshape(depth, b, 6, d)


def _norm_kernel(x_ref, mod_ref, o_ref, *, shift, scale):
    x = x_ref[...]
    sh = mod_ref[0, shift:shift + 1, :]
    sc = mod_ref[0, scale:scale + 1, :]
    o_ref[...] = (_rms(x) * (1.0 + sc) + sh).astype(o_ref.dtype)


def _prenorm(x2, mod_l, seq, shift, scale):
    t, d = x2.shape
    bm = _tile(seq, 512)
    per_b = seq // bm
    return pl.pallas_call(
        functools.partial(_norm_kernel, shift=shift, scale=scale),
        grid=(t // bm,),
        in_specs=[pl.BlockSpec((bm, d), lambda i: (i, 0)),
                  pl.BlockSpec((1, 6, d), lambda i: (i // per_b, 0, 0))],
        out_specs=pl.BlockSpec((bm, d), lambda i: (i, 0)),
        out_shape=jax.ShapeDtypeStruct((t, d), BF16),
        compiler_params=_params(("parallel",), 32),
        name="prenorm",
    )(x2, mod_l)


def _mm_kernel(a_ref, w_ref, b_ref, o_ref):
    acc = jnp.dot(a_ref[...], w_ref[...], preferred_element_type=F32)
    o_ref[...] = (acc + b_ref[...]).astype(o_ref.dtype)


def _matmul_bias(a, w, bias, out_dtype, bm_pref=1024, bn_pref=1024):
    m, k = a.shape
    n = w.shape[1]
    bm = _tile(m, bm_pref)
    bn = _tile(n, bn_pref)
    return pl.pallas_call(
        _mm_kernel,
        grid=(n // bn, m // bm),
        in_specs=[pl.BlockSpec((bm, k), lambda j, i: (i, 0)),
                  pl.BlockSpec((k, bn), lambda j, i: (0, j)),
                  pl.BlockSpec((1, bn), lambda j, i: (0, j))],
        out_specs=pl.BlockSpec((bm, bn), lambda j, i: (i, j)),
        out_shape=jax.ShapeDtypeStruct((m, n), out_dtype),
        compiler_params=_params(("parallel", "arbitrary"), 48),
        name="matmul_bias",
    )(a, w, bias)


CONV_ROWS = 32
CONV_COLS = 256


def _fill_halo(xs_ref, prev, cur, nxt, first, last):
    hb = prev.shape[0]
    bm = cur.shape[0]
    xs_ref[0:hb, :] = jnp.where(first, 0.0, prev)
    xs_ref[hb:hb + bm, :] = cur
    xs_ref[hb + bm:hb + bm + hb, :] = jnp.where(last, 0.0, nxt)


def _conv_rows(xs_ref, w_ref, emit, *, taps, bm, width):
    pad = taps // 2
    lead = BF16_TILE_ROWS - pad
    span = CONV_ROWS + 2 * BF16_TILE_ROWS
    cb = min(CONV_COLS, width)
    assert bm % CONV_ROWS == 0 and width % cb == 0 and lead >= 0

    def body(rb, carry):
        r0 = pl.multiple_of(rb * CONV_ROWS, CONV_ROWS)
        for c0 in range(0, width, cb):
            blk = xs_ref[pl.ds(r0, span), c0:c0 + cb]
            acc = jnp.zeros((CONV_ROWS, cb), F32)
            for k in range(taps):
                acc = acc + w_ref[k:k + 1, c0:c0 + cb] * blk[lead + k:lead + k + CONV_ROWS, :]
            emit(r0, c0, acc)
        return carry

    lax.fori_loop(0, bm // CONV_ROWS, body, 0)


def _qkconv_kernel(prev_ref, cur_ref, next_ref, w_ref, o_ref, xs_ref, *, taps):
    i = pl.program_id(2)
    bm, width = cur_ref.shape[1], cur_ref.shape[2]
    _fill_halo(xs_ref, prev_ref[0].astype(F32), cur_ref[0].astype(F32), next_ref[0].astype(F32),
               i == 0, i == pl.num_programs(2) - 1)

    def emit(r0, c0, acc):
        o_ref[0, pl.ds(r0, CONV_ROWS), c0:c0 + acc.shape[1]] = acc.astype(o_ref.dtype)

    _conv_rows(xs_ref, w_ref, emit, taps=taps, bm=bm, width=width)


def _halo_specs(bm, cb, seq, col0):
    hb = BF16_TILE_ROWS
    per = bm // hb
    n_hb = seq // hb
    return [pl.BlockSpec((1, hb, cb), lambda b, j, i: (b, jnp.maximum(i * per - 1, 0), col0 + j)),
            pl.BlockSpec((1, bm, cb), lambda b, j, i: (b, i, col0 + j)),
            pl.BlockSpec((1, hb, cb), lambda b, j, i: (b, jnp.minimum((i + 1) * per, n_hb - 1), col0 + j))]


def _qk_conv(z3, w_qk, width):
    bsz, seq, _ = z3.shape
    taps = w_qk.shape[0]
    bm = _tile(seq, 512)
    cb = _tile(width, 2048)
    return pl.pallas_call(
        functools.partial(_qkconv_kernel, taps=taps),
        grid=(bsz, width // cb, seq // bm),
        in_specs=_halo_specs(bm, cb, seq, 0) + [pl.BlockSpec((taps, cb), lambda b, j, i: (0, j))],
        out_specs=pl.BlockSpec((1, bm, cb), lambda b, j, i: (b, i, j)),
        out_shape=jax.ShapeDtypeStruct((bsz, seq, width), BF16),
        scratch_shapes=[pltpu.VMEM((bm + 2 * BF16_TILE_ROWS, cb), F32)],
        compiler_params=_params(("parallel", "parallel", "arbitrary"), 32),
        name="qk_conv",
    )(z3, z3, z3, w_qk)


CONV_TOKENS = 8


def _glu_conv_kernel(prev_ref, cur_ref, next_ref, w_ref, b_ref, o_ref, u_ref, *, taps):
    i = pl.program_id(1)
    bm = cur_ref.shape[1]
    g = cur_ref.shape[2] // 2
    hb = BF16_TILE_ROWS
    lead = hb - taps // 2
    assert bm % CONV_TOKENS == 0 and lead >= 0

    def glu(blk):
        x = blk.astype(F32)
        return x[:, :g, :] * _sigmoid(x[:, g:, :])

    u_ref[0:hb] = jnp.where(i == 0, 0.0, glu(prev_ref[0]))
    u_ref[hb:hb + bm] = glu(cur_ref[0])
    u_ref[hb + bm:hb + bm + hb] = jnp.where(i == pl.num_programs(1) - 1, 0.0, glu(next_ref[0]))

    def body(grp, carry):
        t0 = grp * CONV_TOKENS
        acc = [b_ref[...]] * CONV_TOKENS
        for j in range(CONV_TOKENS + taps - 1):
            xin = u_ref[t0 + lead + j]
            for o in range(CONV_TOKENS):
                if 0 <= j - o < taps:
                    acc[o] = acc[o] + w_ref[j - o] * xin
        for o in range(CONV_TOKENS):
            o_ref[0, t0 + o] = acc[o]
        return carry

    lax.fori_loop(0, bm // CONV_TOKENS, body, 0)


def _glu_conv(glu4, w_dw, b_dw):
    bsz, seq, g2, _ = glu4.shape
    g = g2 // 2
    taps = w_dw.shape[0]
    bm = _tile(seq, 512)
    hb = BF16_TILE_ROWS
    per = bm // hb
    n_hb = seq // hb
    return pl.pallas_call(
        functools.partial(_glu_conv_kernel, taps=taps),
        grid=(bsz, seq // bm),
        in_specs=[pl.BlockSpec((1, hb, g2, LANES), lambda b, i: (b, jnp.maximum(i * per - 1, 0), 0, 0)),
                  pl.BlockSpec((1, bm, g2, LANES), lambda b, i: (b, i, 0, 0)),
                  pl.BlockSpec((1, hb, g2, LANES), lambda b, i: (b, jnp.minimum((i + 1) * per, n_hb - 1), 0, 0)),
                  pl.BlockSpec((taps, g, LANES), lambda b, i: (0, 0, 0)),
                  pl.BlockSpec((g, LANES), lambda b, i: (0, 0))],
        out_specs=pl.BlockSpec((1, bm, g, LANES), lambda b, i: (b, i, 0, 0)),
        out_shape=jax.ShapeDtypeStruct((bsz, seq, g, LANES), F32),
        scratch_shapes=[pltpu.VMEM((bm + 2 * hb, g, LANES), F32)],
        compiler_params=_params(("parallel", "arbitrary"), 32),
        name="glu_conv",
    )(glu4, glu4, glu4, w_dw.reshape(taps, g, LANES), b_dw.reshape(g, LANES))


CHUNKS_PER_STEP = 2
FWD, BWD = 0, 1


def _split2(x):
    hi = x.astype(BF16).astype(F32)
    lo = (x - hi).astype(BF16).astype(F32)
    return jnp.concatenate([hi, lo], axis=1)


def _column_forms(eye2, piece_rows):
    rhs = jnp.concatenate([jnp.broadcast_to(p, (LANES, p.shape[1])) for p in piece_rows], axis=0)
    return lax.dot_general(eye2, rhs.astype(BF16), (((1,), (1,)), ((), ())), preferred_element_type=F32)


def _mlstm_direction(q, k, v, a_row, a_col, i_col, b_tot, m_prev, m_new, c_ref, n_ref, bias):
    reps = q.shape[1] // LANES
    wide = lambda x: jnp.concatenate([x] * reps, axis=1)
    b_col = a_col + i_col
    dmat = (b_col - a_row) + bias
    inter = b_col + m_prev
    m_t = jnp.maximum(inter, jnp.max(dmat, axis=-1, keepdims=True))
    w_intra = jnp.exp(dmat - m_t)
    w_inter = jnp.exp(inter - m_t)
    s = lax.dot_general(q, k, (((1,), (1,)), ((), ())), preferred_element_type=F32) * w_intra
    c_prev = c_ref[...]
    n_prev = n_ref[...]
    num = jnp.dot(s.astype(BF16), v, preferred_element_type=F32) + \
        wide(w_inter) * jnp.dot(q, c_prev.astype(BF16), preferred_element_type=F32)
    den = jnp.sum(s, axis=-1, keepdims=True) + \
        w_inter * jnp.sum(q.astype(F32) * n_prev, axis=-1, keepdims=True)
    h = num * wide(1.0 / jnp.maximum(jnp.abs(den), jnp.exp(-m_t)))
    w_s = jnp.exp(b_tot - m_new - a_col)
    decay = wide(jnp.exp(b_tot + m_prev - m_new))
    kw = k.astype(F32) * wide(w_s)
    c_ref[...] = decay * c_prev + lax.dot_general(kw.astype(BF16), v, (((0,), (0,)), ((), ())),
                                                  preferred_element_type=F32)
    n_ref[...] = decay * n_prev + jnp.sum(kw, axis=0, keepdims=True)
    return h


def _mlstm_kernel(q_ref, k_ref, v_ref, o_ref, grow_ref, g_ref, out_ref,
                  hs_ref, a_ref, a2_ref, i2_ref, btot_ref, peak_ref, mprev_ref, mnew_ref,
                  eye_ref, bias_ref, c_ref, n_ref):
    length = CHUNK
    seq = q_ref.shape[1]
    n_chunks = seq // length
    row = lax.broadcasted_iota(I32, (length, length), 0)
    col = lax.broadcasted_iota(I32, (length, length), 1)
    lower = row >= col
    upper = row <= col
    lower_f = jnp.where(lower, 1.0, 0.0)
    upper_f = jnp.where(upper, 1.0, 0.0)
    eye_f = jnp.where(row == col, 1.0, 0.0)
    eye_ref[...] = jnp.concatenate([eye_f, eye_f], axis=1).astype(BF16)
    bias_ref[FWD] = jnp.where(lower, 0.0, NEG_INF)
    bias_ref[BWD] = jnp.where(upper, 0.0, NEG_INF)
    c_ref[...] = jnp.zeros(c_ref.shape, F32)
    n_ref[...] = jnp.zeros(n_ref.shape, F32)

    for d, cum_row, last in ((FWD, upper_f, length - 1), (BWD, lower_f, 0)):
        gate_i = grow_ref[0, 0, 2 * d]
        lf = _log_sigmoid(grow_ref[0, 0, 2 * d + 1])
        b_row = jnp.dot(lf, cum_row, preferred_element_type=F32, precision=HIGHEST)
        a = b_row - gate_i
        a_ref[d] = a
        a2_ref[d] = _split2(a)
        i2_ref[d] = _split2(gate_i)
        btot_ref[d] = jnp.broadcast_to(b_row[:, last:last + 1], (n_chunks, LANES))
        peak_ref[d] = jnp.broadcast_to(jnp.max(-a, axis=-1, keepdims=True), (n_chunks, LANES))

    def stabiliser_scan(c, carry):
        new = []
        for d, m in zip((FWD, BWD), carry):
            i = c if d == FWD else n_chunks - 1 - c
            mprev_ref[d, pl.ds(i, 1), :] = m
            m = btot_ref[d, pl.ds(i, 1), :] + jnp.maximum(m, peak_ref[d, pl.ds(i, 1), :])
            mnew_ref[d, pl.ds(i, 1), :] = m
            new.append(m)
        return tuple(new)

    zero = jnp.zeros((1, LANES), F32)
    lax.fori_loop(0, n_chunks, stabiliser_scan, (zero, zero))

    def run(d, i, a_col, i_col):
        r0 = pl.multiple_of(i * length, length)
        one = pl.ds(i, 1)
        h = _mlstm_direction(q_ref[0, pl.ds(r0, length), :], k_ref[0, pl.ds(r0, length), :],
                             v_ref[0, pl.ds(r0, length), :], a_ref[d, one, :], a_col, i_col,
                             btot_ref[d, one, :], mprev_ref[d, one, :], mnew_ref[d, one, :],
                             c_ref.at[d], n_ref.at[d], bias_ref[d])
        return r0, h

    def chunk_results(step):
        work = []
        for u in range(CHUNKS_PER_STEP):
            c = step * CHUNKS_PER_STEP + u
            work += [(FWD, c), (BWD, n_chunks - 1 - c)]
        rows = []
        for d, i in work:
            rows += [a2_ref[d, pl.ds(i, 1), :], i2_ref[d, pl.ds(i, 1), :]]
        cols = _column_forms(eye_ref[...], rows)
        lanes = lambda j: cols[:, j * LANES:(j + 1) * LANES]
        return [run(d, i, lanes(2 * j), lanes(2 * j + 1)) for j, (d, i) in enumerate(work)]

    def first_touch(step, carry):
        for r0, h in chunk_results(step):
            hs_ref[pl.ds(r0, length), :] = h
        return carry

    def second_touch(step, carry):
        for r0, h in chunk_results(step):
            hs = hs_ref[pl.ds(r0, length), :] + h
            gate = _sigmoid(o_ref[0, pl.ds(r0, length), :].astype(F32))
            out_ref[0, pl.ds(r0, length), :] = ((_rms(hs) * g_ref[...]) * gate).astype(out_ref.dtype)
        return carry

    steps = n_chunks // CHUNKS_PER_STEP
    lax.fori_loop(0, steps // 2, first_touch, 0)
    lax.fori_loop(steps // 2, steps, second_touch, 0)


def _mlstm_branch(qk3, z3, gates, m_norm_g, heads):
    bsz, seq, _ = z3.shape
    dh = HEAD_DIM
    wm = heads * dh
    n_chunks = seq // CHUNK
    assert CHUNK == LANES and dh % LANES == 0
    assert seq % CHUNK == 0 and n_chunks % (2 * CHUNKS_PER_STEP) == 0
    grow = gates.reshape(bsz, seq, 2, 2, heads).transpose(0, 4, 2, 3, 1).reshape(bsz, heads, 4, n_chunks, CHUNK)
    seq_blk = lambda col0: pl.BlockSpec((1, seq, dh), lambda b, h: (b, 0, col0 + h))
    per_chunk = lambda w: pltpu.VMEM((2, n_chunks, w), F32)
    return pl.pallas_call(
        _mlstm_kernel,
        grid=(bsz, heads),
        in_specs=[seq_blk(0), seq_blk(heads), seq_blk(2 * heads), seq_blk(3 * heads),
                  pl.BlockSpec((1, 1, 4, n_chunks, CHUNK), lambda b, h: (b, h, 0, 0, 0)),
                  pl.BlockSpec((1, dh), lambda b, h: (0, h))],
        out_specs=pl.BlockSpec((1, seq, dh), lambda b, h: (b, 0, h)),
        out_shape=jax.ShapeDtypeStruct((bsz, seq, wm), BF16),
        scratch_shapes=[pltpu.VMEM((seq, dh), F32),
                        per_chunk(CHUNK), per_chunk(2 * CHUNK), per_chunk(2 * CHUNK),
                        per_chunk(LANES), per_chunk(LANES), per_chunk(LANES), per_chunk(LANES),
                        pltpu.VMEM((CHUNK, 2 * CHUNK), BF16), pltpu.VMEM((2, CHUNK, CHUNK), F32),
                        pltpu.VMEM((2, dh, dh), F32), pltpu.VMEM((2, 1, dh), F32)],
        compiler_params=_params(("parallel", "parallel"), 58),
        name="mlstm",
    )(qk3, qk3, z3, z3, grow, m_norm_g.reshape(1, wm))


def _merge_kernel(hm_ref, uc_ref, ln_ref, gm_ref, gc_ref, x_ref, mod_ref, wm_ref, wc_ref, wo_ref, *rest, route):
    if route:
        rt_ref, xn_ref, hp_ref, lg_ref = rest
    else:
        xn_ref, hp_ref = rest
    y_m = jnp.dot(hm_ref[...], wm_ref[...], preferred_element_type=F32)
    yc = uc_ref[...]
    yc = yc - jnp.mean(yc, axis=-1, keepdims=True)
    var = jnp.mean(yc * yc, axis=-1, keepdims=True)
    u = _silu(yc * lax.rsqrt(var + EPS) * ln_ref[0:1, :] + ln_ref[1:2, :])
    y_c = jnp.dot(u.astype(BF16), wc_ref[...], preferred_element_type=F32)
    merged = _sigmoid(gm_ref[...].astype(F32)) * y_m + _sigmoid(gc_ref[...].astype(F32)) * y_c
    out = jnp.dot(merged.astype(BF16), wo_ref[...], preferred_element_type=F32)
    x_new = x_ref[...] + mod_ref[0, 2:3, :] * out
    xn_ref[...] = x_new
    h = _rms(x_new) * (1.0 + mod_ref[0, 4:5, :]) + mod_ref[0, 3:4, :]
    hp_ref[...] = _pack_halves(h)
    if route:
        h_hi = h.astype(BF16)
        h_lo = (h - h_hi.astype(F32)).astype(BF16)
        lg_ref[...] = (jnp.dot(h_hi, rt_ref[0], preferred_element_type=F32)
                       + jnp.dot(h_lo, rt_ref[0], preferred_element_type=F32)
                       + jnp.dot(h_hi, rt_ref[1], preferred_element_type=F32))


def _merge(hm, uc, ln_gb, z2, col_gm, x2, mod_l, w_m, w_c, w_o, seq, router_t):
    t, d = x2.shape
    wm_, wc_ = hm.shape[1], uc.shape[1]
    bm = _tile(seq, 256)
    per_b = seq // bm
    assert col_gm % d == 0
    blk = col_gm // d
    route = router_t is not None
    once = pl.Buffered(1)
    in_specs = [pl.BlockSpec((bm, wm_), lambda i: (i, 0)),
                pl.BlockSpec((bm, wc_), lambda i: (i, 0)),
                pl.BlockSpec((2, wc_), lambda i: (0, 0)),
                pl.BlockSpec((bm, d), lambda i: (i, blk)),
                pl.BlockSpec((bm, d), lambda i: (i, blk + 1)),
                pl.BlockSpec((bm, d), lambda i: (i, 0)),
                pl.BlockSpec((1, 6, d), lambda i: (i // per_b, 0, 0)),
                pl.BlockSpec((wm_, d), lambda i: (0, 0), pipeline_mode=once),
                pl.BlockSpec((wc_, d), lambda i: (0, 0), pipeline_mode=once),
                pl.BlockSpec((d, d), lambda i: (0, 0), pipeline_mode=once)]
    out_specs = [pl.BlockSpec((bm, d), lambda i: (i, 0)), pl.BlockSpec((bm, d // 2), lambda i: (i, 0))]
    out_shape = [jax.ShapeDtypeStruct((t, d), F32), jax.ShapeDtypeStruct((t, d // 2), U32)]
    args = [hm, uc, ln_gb, z2, z2, x2, mod_l, w_m, w_c, w_o]
    if route:
        in_specs.append(pl.BlockSpec((2, d, LANES), lambda i: (0, 0, 0)))
        out_specs.append(pl.BlockSpec((bm, LANES), lambda i: (i, 0)))
        out_shape.append(jax.ShapeDtypeStruct((t, LANES), F32))
        args.append(router_t)
    return pl.pallas_call(
        functools.partial(_merge_kernel, route=route),
        grid=(t // bm,),
        in_specs=in_specs, out_specs=out_specs, out_shape=out_shape,
        compiler_params=_params(("parallel",), 48),
        name="merge",
    )(*args)


FFN_ROWS = 512
DENSE_ROWS = 1024
FIRST_OF_GROUP = 1
VALID = 2


CAST_ROWS = 512


def _stage_group_weights(te_ref, tn_ref, w_hbm, stage_ref, wb_ref, sem):
    n = pl.program_id(0)
    r = pl.program_id(1)
    k_rows, bn = stage_ref.shape[1], stage_ref.shape[2]
    half = w_hbm.shape[2] // 2

    def fetch(e, col_block):
        c0 = col_block * bn
        return [pltpu.make_async_copy(w_hbm.at[e, :, pl.ds(pl.multiple_of(c0 + k * half, LANES), bn)],
                                      stage_ref.at[k], sem.at[k]) for k in range(2)]

    @pl.when((n == 0) & (r == 0))
    def _():
        for cp in fetch(te_ref[0], 0):
            cp.start()

    for cp in fetch(te_ref[r], n):
        cp.wait()

    step_rows = math.gcd(k_rows, CAST_ROWS)

    def cast(i, carry):
        rows = pl.ds(pl.multiple_of(i * step_rows, step_rows), step_rows)
        wb_ref[:, rows, :] = stage_ref[:, rows, :].astype(BF16)
        return carry

    lax.fori_loop(0, k_rows // step_rows, cast, 0)
    nxt = tn_ref[r]

    @pl.when(nxt >= 0)
    def _():
        for cp in fetch(nxt, n):
            cp.start()

    @pl.when((nxt < 0) & (n + 1 < pl.num_programs(0)))
    def _():
        for cp in fetch(te_ref[0], n + 1):
            cp.start()


def _group_scratch(k_rows, bn):
    assert math.gcd(k_rows, CAST_ROWS) % BF16_TILE_ROWS == 0
    return [pltpu.VMEM((2, k_rows, bn), F32), pltpu.VMEM((2, k_rows, bn), BF16), pltpu.SemaphoreType.DMA((2,))]


def _ffn_up_kernel(te_ref, tf_ref, tn_ref, x_ref, w_hbm, o_ref, stage_ref, wb_ref, sem):
    flag = tf_ref[pl.program_id(1)]

    @pl.when((flag & FIRST_OF_GROUP) != 0)
    def _():
        _stage_group_weights(te_ref, tn_ref, w_hbm, stage_ref, wb_ref, sem)

    @pl.when((flag & VALID) != 0)
    def _():
        lo, hi = _unpack_halves(x_ref[...])
        x = jnp.concatenate([lo.astype(BF16), hi.astype(BF16)], axis=1)
        a = jnp.dot(x, wb_ref[0], preferred_element_type=F32)
        g = jnp.dot(x, wb_ref[1], preferred_element_type=F32)
        o_ref[...] = (_silu(a) * g).astype(o_ref.dtype)

    @pl.when((flag & VALID) == 0)
    def _():
        o_ref[...] = jnp.zeros(o_ref.shape, o_ref.dtype)


def _ffn_up(xp, w13, tile_e, tile_f, tile_next, bm):
    rows, dh = xp.shape
    n_e, d, f2 = w13.shape
    f = f2 // 2
    bn = _tile(f, 1024)
    return pl.pallas_call(
        _ffn_up_kernel,
        grid_spec=pltpu.PrefetchScalarGridSpec(
            num_scalar_prefetch=3,
            grid=(f // bn, rows // bm),
            in_specs=[pl.BlockSpec((bm, dh), lambda j, r, te, tf, tn: (r, 0)),
                      pl.BlockSpec(memory_space=pl.ANY)],
            out_specs=pl.BlockSpec((bm, bn), lambda j, r, te, tf, tn: (r, j)),
            scratch_shapes=_group_scratch(d, bn)),
        out_shape=jax.ShapeDtypeStruct((rows, f), BF16),
        compiler_params=_params(("arbitrary", "arbitrary"), 48),
        name="ffn_up",
    )(tile_e, tile_f, tile_next, xp, w13)


def _ffn_down_kernel(te_ref, tf_ref, tn_ref, h_ref, w_hbm, o_ref, stage_ref, wb_ref, sem):
    flag = tf_ref[pl.program_id(1)]

    @pl.when((flag & FIRST_OF_GROUP) != 0)
    def _():
        _stage_group_weights(te_ref, tn_ref, w_hbm, stage_ref, wb_ref, sem)

    @pl.when((flag & VALID) != 0)
    def _():
        h = h_ref[...]
        ya = jnp.dot(h, wb_ref[0], preferred_element_type=F32)
        yb = jnp.dot(h, wb_ref[1], preferred_element_type=F32)
        o_ref[...] = _pack_halves(jnp.concatenate([ya, yb], axis=1))

    @pl.when((flag & VALID) == 0)
    def _():
        o_ref[...] = jnp.zeros(o_ref.shape, o_ref.dtype)


def _ffn_down(h, w2, tile_e, tile_f, tile_next, bm):
    rows, f = h.shape
    n_e, _, d = w2.shape
    dh = d // 2
    bn = _tile(dh, 256)
    nn = dh // bn
    return pl.pallas_call(
        _ffn_down_kernel,
        grid_spec=pltpu.PrefetchScalarGridSpec(
            num_scalar_prefetch=3,
            grid=(nn, rows // bm),
            in_specs=[pl.BlockSpec((bm, f), lambda n, r, te, tf, tn: (r, 0)),
                      pl.BlockSpec(memory_space=pl.ANY)],
            out_specs=pl.BlockSpec((bm, bn), lambda n, r, te, tf, tn: (r, n)),
            scratch_shapes=_group_scratch(f, bn)),
        out_shape=jax.ShapeDtypeStruct((rows, dh), U32),
        compiler_params=_params(("arbitrary", "arbitrary"), 48),
        name="ffn_down",
    )(tile_e, tile_f, tile_next, h, w2)


def _residual_epilogue(x, gate, f, mod_ref, fin_ref, xn_ref, hn_ref, final):
    x_new = x + gate * f
    if final:
        xn_ref[...] = _rms(x_new) * fin_ref[...]
    else:
        xn_ref[...] = x_new
        hn_ref[...] = (_rms(x_new) * (1.0 + mod_ref[0, 1:2, :]) + mod_ref[0, 0:1, :]).astype(hn_ref.dtype)


def _dense_residual_kernel(y_ref, x_ref, mod_ref, nmod_ref, fin_ref, xn_ref, *rest, final):
    hn_ref = None if final else rest[0]
    lo, hi = _unpack_halves(y_ref[...])
    f = jnp.concatenate([lo, hi], axis=1)
    _residual_epilogue(x_ref[...], mod_ref[0, 5:6, :], f, nmod_ref, fin_ref, xn_ref, hn_ref, final)


def _dense_residual(yp, x2, mod_l, mod_next, final_g, seq, final):
    t, d = x2.shape
    bm = _tile(seq, 512)
    per_b = seq // bm
    mod_spec = pl.BlockSpec((1, 6, d), lambda i: (i // per_b, 0, 0))
    row = lambda w: pl.BlockSpec((bm, w), lambda i: (i, 0))
    out_specs = [row(d)] + ([] if final else [row(d)])
    out_shape = [jax.ShapeDtypeStruct((t, d), F32)] + ([] if final else [jax.ShapeDtypeStruct((t, d), BF16)])
    return pl.pallas_call(
        functools.partial(_dense_residual_kernel, final=final),
        grid=(t // bm,),
        in_specs=[row(d // 2), row(d), mod_spec, mod_spec, pl.BlockSpec((1, d), lambda i: (0, 0))],
        out_specs=out_specs, out_shape=out_shape,
        compiler_params=_params(("parallel",), 40),
        name="dense_residual",
    )(yp, x2, mod_l, mod_next, final_g.reshape(1, d))


ROUTE_TOKENS = 512


def _route_kernel(lg_ref, idx_ref, wt_ref, cnt_ref, run_ref):
    step = pl.program_id(0)
    n_e, tb = lg_ref.shape

    @pl.when(step == 0)
    def _():
        run_ref[...] = jnp.zeros(run_ref.shape, F32)

    lg = lg_ref[...]
    eid = lax.broadcasted_iota(I32, (n_e, tb), 0)
    m1 = jnp.max(lg, axis=0, keepdims=True)
    e1 = jnp.min(jnp.where(lg == m1, eid, n_e), axis=0, keepdims=True)
    rest = jnp.where(eid == e1, NEG_INF, lg)
    m2 = jnp.max(rest, axis=0, keepdims=True)
    e2 = jnp.min(jnp.where(rest == m2, eid, n_e), axis=0, keepdims=True)
    p2 = jnp.exp(m2 - m1)
    w1 = 1.0 / (1.0 + p2)
    w2 = p2 / (1.0 + p2)
    member = jnp.where((eid == e1) | (eid == e2), 1.0, 0.0)
    before = (lax.broadcasted_iota(I32, (tb, tb), 0) < lax.broadcasted_iota(I32, (tb, tb), 1)).astype(BF16)
    rank = jnp.dot(member.astype(BF16), before, preferred_element_type=F32) + run_ref[:, 0:1]
    r1 = jnp.sum(jnp.where(eid == e1, rank, 0.0), axis=0, keepdims=True)
    r2 = jnp.sum(jnp.where(eid == e2, rank, 0.0), axis=0, keepdims=True)
    zero_i = jnp.zeros((SUBLANES - 4, tb), I32)
    idx_ref[...] = jnp.concatenate([e1, e2, r1.astype(I32), r2.astype(I32), zero_i], axis=0)
    wt_ref[...] = jnp.concatenate([w1, w2, jnp.zeros((SUBLANES - 2, tb), F32)], axis=0)
    run_ref[...] = run_ref[...] + jnp.sum(member, axis=1, keepdims=True)
    cnt_ref[...] = run_ref[...]


def _route(logits_t):
    n_e, t = logits_t.shape
    assert n_e % SUBLANES == 0
    tb = _tile(t, ROUTE_TOKENS)
    return pl.pallas_call(
        _route_kernel,
        grid=(t // tb,),
        in_specs=[pl.BlockSpec((n_e, tb), lambda i: (0, i))],
        out_specs=[pl.BlockSpec((SUBLANES, tb), lambda i: (0, i)),
                   pl.BlockSpec((SUBLANES, tb), lambda i: (0, i)),
                   pl.BlockSpec((n_e, LANES), lambda i: (0, 0))],
        out_shape=[jax.ShapeDtypeStruct((SUBLANES, t), I32), jax.ShapeDtypeStruct((SUBLANES, t), F32),
                   jax.ShapeDtypeStruct((n_e, LANES), F32)],
        scratch_shapes=[pltpu.VMEM((n_e, LANES), F32)],
        compiler_params=_params(("arbitrary",), 32),
        name="route",
    )(logits_t)


MOVE_TOKENS = 256
ISSUE_UNROLL = 8


def _row_copy(src, dst, sem):
    return pltpu.make_async_copy(src, dst, sem)


def _dispatch_kernel(dest_ref, x_ref, init_ref, out_ref, sem):
    del init_ref
    n_tok = x_ref.shape[0]
    total = dest_ref.shape[0] // TOP_K
    base = pl.program_id(0) * n_tok

    def issue(i, carry):
        for k in range(TOP_K):
            _row_copy(x_ref.at[pl.ds(i, 1)], out_ref.at[pl.ds(dest_ref[k * total + base + i], 1)],
                      sem).start(priority=k % 2)
        return carry

    lax.fori_loop(0, n_tok, issue, 0, unroll=ISSUE_UNROLL)
    for k in range(TOP_K):
        _row_copy(x_ref, out_ref.at[pl.ds(0, n_tok)], sem).wait()


def _dispatch(xp, dest_flat, rows):
    t, dh = xp.shape
    nt = _tile(t, MOVE_TOKENS)
    return pl.pallas_call(
        _dispatch_kernel,
        grid_spec=pltpu.PrefetchScalarGridSpec(
            num_scalar_prefetch=1,
            grid=(t // nt,),
            in_specs=[pl.BlockSpec((nt, dh), lambda i, dest: (i, 0)),
                      pl.BlockSpec(memory_space=pl.ANY)],
            out_specs=pl.BlockSpec(memory_space=pl.ANY),
            scratch_shapes=[pltpu.SemaphoreType.DMA(())]),
        out_shape=jax.ShapeDtypeStruct((rows, dh), U32),
        input_output_aliases={2: 0},
        compiler_params=_params(("arbitrary",), 32),
        name="dispatch",
    )(dest_flat, xp, jnp.zeros((rows, dh), U32))


def _combine_kernel(dest_ref, y_ref, wt_ref, x_ref, mod_ref, nmod_ref, fin_ref, xn_ref, *rest, final):
    if final:
        hn_ref = None
        buf_ref, sem = rest
    else:
        hn_ref, buf_ref, sem = rest
    n_tok = x_ref.shape[0]
    total = dest_ref.shape[0] // TOP_K
    step = pl.program_id(0)

    def gather(s):
        slot = s % 2

        def issue(i, carry):
            for k in range(TOP_K):
                _row_copy(y_ref.at[pl.ds(dest_ref[k * total + s * n_tok + i], 1)],
                          buf_ref.at[slot, k, pl.ds(i, 1)], sem.at[slot]).start(priority=k % 2)
            return carry

        lax.fori_loop(0, n_tok, issue, 0, unroll=ISSUE_UNROLL)

    @pl.when(step == 0)
    def _():
        gather(step)

    @pl.when(step + 1 < pl.num_programs(0))
    def _():
        gather(step + 1)

    slot = step % 2
    for k in range(TOP_K):
        _row_copy(y_ref.at[pl.ds(0, n_tok)], buf_ref.at[slot, k], sem.at[slot]).wait()
    f = None
    for k in range(TOP_K):
        lo, hi = _unpack_halves(buf_ref[slot, k])
        part = wt_ref[:, k:k + 1] * jnp.concatenate([lo, hi], axis=1)
        f = part if f is None else f + part
    _residual_epilogue(x_ref[...], mod_ref[0, 5:6, :], f, nmod_ref, fin_ref, xn_ref, hn_ref, final)


def _combine(yp, dest_flat, wt_cols, x2, mod_l, mod_next, final_g, seq, final):
    t, d = x2.shape
    dh = d // 2
    nt = _tile(seq, MOVE_TOKENS)
    per_b = seq // nt
    mod_spec = pl.BlockSpec((1, 6, d), lambda i, dest: (i // per_b, 0, 0))
    row = lambda w: pl.BlockSpec((nt, w), lambda i, dest: (i, 0))
    out_specs = [row(d)] + ([] if final else [row(d)])
    out_shape = [jax.ShapeDtypeStruct((t, d), F32)] + ([] if final else [jax.ShapeDtypeStruct((t, d), BF16)])
    return pl.pallas_call(
        functools.partial(_combine_kernel, final=final),
        grid_spec=pltpu.PrefetchScalarGridSpec(
            num_scalar_prefetch=1,
            grid=(t // nt,),
            in_specs=[pl.BlockSpec(memory_space=pl.ANY), row(TOP_K), row(d), mod_spec, mod_spec,
                      pl.BlockSpec((1, d), lambda i, dest: (0, 0))],
            out_specs=out_specs,
            scratch_shapes=[pltpu.VMEM((2, TOP_K, nt, dh), U32), pltpu.SemaphoreType.DMA((2,))]),
        out_shape=out_shape,
        compiler_params=_params(("arbitrary",), 40),
        name="combine",
    )(dest_flat, yp, wt_cols, x2, mod_l, mod_next, final_g.reshape(1, d))


def _dense_ffn(hp, x2, w13, w2, mod_l, mod_next, final_g, seq, final):
    t = hp.shape[0]
    bm = math.gcd(t, DENSE_ROWS)
    n_tiles = t // bm
    tile_e = jnp.zeros((n_tiles,), I32)
    tile_f = jnp.full((n_tiles,), VALID, I32).at[0].set(VALID | FIRST_OF_GROUP)
    tile_next = jnp.full((n_tiles,), -1, I32)
    hid = _ffn_up(hp, w13[None], tile_e, tile_f, tile_next, bm)
    yp = _ffn_down(hid, w2[None], tile_e, tile_f, tile_next, bm)
    return _dense_residual(yp, x2, mod_l, mod_next, final_g, seq, final)


def _moe_ffn(hp, logits_t, x2, w13, w2, mod_l, mod_next, final_g, seq, final):
    t = hp.shape[0]
    n_e = w13.shape[0]
    idx, wts, cnt = _route(logits_t)
    counts = cnt[:, 0].astype(I32)
    padded = ((counts + FFN_ROWS - 1) // FFN_ROWS) * FFN_ROWS
    seg_end = jnp.cumsum(padded)
    seg_start = seg_end - padded
    n_tiles = (t * TOP_K) // FFN_ROWS + n_e
    tile_row0 = jnp.arange(n_tiles, dtype=I32) * FFN_ROWS
    tile_e = jnp.minimum(jnp.searchsorted(seg_end, tile_row0, side="right"), n_e - 1).astype(I32)
    valid = tile_row0 < seg_end[-1]
    first = jnp.concatenate([jnp.ones((1,), bool), tile_e[1:] != tile_e[:-1]]) & valid
    tile_f = (jnp.where(valid, VALID, 0) | jnp.where(first, FIRST_OF_GROUP, 0)).astype(I32)
    tile_id = jnp.arange(n_tiles, dtype=I32)
    later_start = lax.cummin(jnp.where(first, tile_id, n_tiles), reverse=True)
    next_start = jnp.concatenate([later_start[1:], jnp.full((1,), n_tiles, I32)])
    tile_next = jnp.where(next_start < n_tiles, tile_e[jnp.minimum(next_start, n_tiles - 1)], -1).astype(I32)
    start_of = jnp.sum(jnp.where(idx[:TOP_K, :, None] == jnp.arange(n_e, dtype=I32), seg_start, 0), axis=-1)
    dest = (start_of + idx[TOP_K:2 * TOP_K]).reshape(-1).astype(I32)
    xs = _dispatch(hp, dest, n_tiles * FFN_ROWS)
    hid = _ffn_up(xs, w13, tile_e, tile_f, tile_next, FFN_ROWS)
    yp = _ffn_down(hid, w2, tile_e, tile_f, tile_next, FFN_ROWS)
    return _combine(yp, dest, wts[:TOP_K].T, x2, mod_l, mod_next, final_g, seq, final)


def kernel(x, c, w_mod, b_mod, w_in, b_in, w_qk_conv, m_norm_g, w_m_proj, w_dw, b_dw, ln_c_g, ln_c_b,
           w_c_proj, w_out, ffn_w13, ffn_w2, moe_router, moe_w13, moe_w2, final_g):
    bsz, seq, d = x.shape
    depth = w_mod.shape[0]
    wm = w_m_proj.shape[1]
    wc = w_dw.shape[2]
    heads = wm // HEAD_DIM
    n_gate = 4 * heads
    off_g = 4 * wm
    off_glu = off_g + n_gate
    t = bsz * seq
    assert d % (2 * LANES) == 0 and n_gate <= LANES

    mods = _adaln_mod(c, w_mod, b_mod)
    x2 = x.reshape(t, d)
    h = _prenorm(x2, mods[0], seq, 0, 1)
    out = None
    for l in range(depth):
        final = l == depth - 1
        mod_l = mods[l]
        mod_next = mods[l] if final else mods[l + 1]
        w_main = jnp.concatenate([w_in[l][:, :off_g], w_in[l][:, off_glu:]], axis=1).astype(BF16)
        b_main = jnp.concatenate([b_in[l][:off_g], b_in[l][off_glu:]])[None]
        w_gate = jnp.zeros((d, LANES), BF16).at[:, :n_gate].set(w_in[l][:, off_g:off_glu].astype(BF16))
        b_gate = jnp.zeros((1, LANES), F32).at[0, :n_gate].set(b_in[l][off_g:off_glu])
        z = _matmul_bias(h, w_main, b_main, BF16)
        gates = _matmul_bias(h, w_gate, b_gate, F32, bn_pref=LANES)[:, :n_gate]
        z3 = z.reshape(bsz, seq, z.shape[1])
        k_scale = jnp.concatenate([jnp.ones((wm,), F32), jnp.full((wm,), HEAD_DIM ** -0.5, F32)])
        qk3 = _qk_conv(z3, w_qk_conv[l] * k_scale[None], 2 * wm)
        hm = _mlstm_branch(qk3, z3, gates.reshape(bsz, seq, n_gate), m_norm_g[l], heads)
        glu4 = z[:, off_g:off_g + 2 * wc].reshape(bsz, seq, 2 * wc // LANES, LANES)
        uc = _glu_conv(glu4, w_dw[l], b_dw[l])
        ln_gb = jnp.stack([ln_c_g[l], ln_c_b[l]])
        moe = l % 2 == 1
        router_t = None
        if moe:
            n_e = moe_router.shape[2]
            r_pad = jnp.zeros((d, LANES), F32).at[:, :n_e].set(moe_router[l // 2])
            r_hi = r_pad.astype(BF16)
            router_t = jnp.stack([r_hi, (r_pad - r_hi.astype(F32)).astype(BF16)])
        res = _merge(hm.reshape(t, wm), uc.reshape(t, wc), ln_gb, z, off_g + 2 * wc, x2, mod_l,
                     w_m_proj[l].astype(BF16), w_c_proj[l].astype(BF16), w_out[l].astype(BF16), seq, router_t)
        if moe:
            x2, hp, logits = res
            res = _moe_ffn(hp, logits[:, :n_e].T, x2, moe_w13[l // 2], moe_w2[l // 2], mod_l, mod_next,
                           final_g, seq, final)
        else:
            x2, hp = res
            res = _dense_ffn(hp, x2, ffn_w13[l // 2], ffn_w2[l // 2], mod_l, mod_next, final_g, seq, final)
        if final:
            out = res[0]
        else:
            x2, h = res
    return out.reshape(bsz, seq, d)
```

```python
import functools
import math

import jax
import jax.numpy as jnp
from jax import lax
from jax.experimental import pallas as pl
from jax.experimental.pallas import tpu as pltpu

F32 = jnp.float32
BF16 = jnp.bfloat16
U32 = jnp.uint32
I32 = jnp.int32

HEAD_DIM = 256
CHUNK = 128
TOP_K = 2
EPS = 1e-6

V7X_VMEM_BYTES = 64 * 1024 * 1024
LANES = 128
SUBLANES = 8
BF16_TILE_ROWS = 16

HIGHEST = lax.Precision.HIGHEST
NEG_INF = float("-inf")


def _params(semantics, vmem_mib):
    assert vmem_mib * 1024 * 1024 < V7X_VMEM_BYTES
    return pltpu.CompilerParams(dimension_semantics=semantics, vmem_limit_bytes=vmem_mib * 1024 * 1024)


def _tile(n, pref):
    if n <= pref:
        return n
    t = pref - pref % LANES
    while t > LANES and n % t:
        t -= LANES
    assert n % t == 0, (n, pref)
    return t


def _sigmoid(x):
    return 1.0 / (1.0 + jnp.exp(-x))


def _silu(x):
    return x * _sigmoid(x)


def _log_sigmoid(x):
    return jnp.minimum(x, 0.0) - jnp.log1p(jnp.exp(-jnp.abs(x)))


def _rms(x):
    return x * lax.rsqrt(jnp.mean(x * x, axis=-1, keepdims=True) + EPS)


def _pack_halves(y):
    n = y.shape[-1] // 2
    lo = lax.bitcast_convert_type(y[:, :n].astype(BF16).astype(F32), U32)
    hi = lax.bitcast_convert_type(y[:, n:].astype(BF16).astype(F32), U32)
    return (hi & jnp.uint32(0xFFFF0000)) | (lo >> 16)


def _unpack_halves(u):
    lo = lax.bitcast_convert_type(u << 16, F32)
    hi = lax.bitcast_convert_type(u & jnp.uint32(0xFFFF0000), F32)
    return lo, hi


def _mod_kernel(c_ref, w_ref, b_ref, o_ref):
    c = _silu(c_ref[...])
    w = w_ref[0]
    c_hi = c.astype(BF16)
    c_lo = (c - c_hi.astype(F32)).astype(BF16)
    w_hi = w.astype(BF16)
    w_lo = (w - w_hi.astype(F32)).astype(BF16)
    o_ref[0] = (jnp.dot(c_hi, w_hi, preferred_element_type=F32) + jnp.dot(c_lo, w_hi, preferred_element_type=F32)
                + jnp.dot(c_hi, w_lo, preferred_element_type=F32) + b_ref[0])


def _adaln_mod(c, w_mod, b_mod):
    depth, d, n = w_mod.shape
    b = c.shape[0]
    c_pad = jnp.zeros((SUBLANES, d), F32).at[:b].set(c)
    bn = _tile(n, 1024)
    out = pl.pallas_call(
        _mod_kernel,
        grid=(depth, n // bn),
        in_specs=[pl.BlockSpec((SUBLANES, d), lambda l, j: (0, 0)),
                  pl.BlockSpec((1, d, bn), lambda l, j: (l, 0, j)),
                  pl.BlockSpec((1, 1, bn), lambda l, j: (l, 0, j))],
        out_specs=pl.BlockSpec((1, SUBLANES, bn), lambda l, j: (l, 0, j)),
        out_shape=jax.ShapeDtypeStruct((depth, SUBLANES, n), F32),
        compiler_params=_params(("arbitrary", "arbitrary"), 40),
        name="adaln_mod",
    )(c_pad, w_mod, b_mod.reshape(depth, 1, n))
    return out[:, :b].reshape(depth, b, 6, d)


def _emit_mixer_input(x, mod_ref, wg_ref, bg_ref, h_ref, gt_ref):
    h = (_rms(x) * (1.0 + mod_ref[0, 1:2, :]) + mod_ref[0, 0:1, :]).astype(BF16)
    h_ref[...] = h
    gt_ref[...] = jnp.dot(h, wg_ref[...], preferred_element_type=F32) + bg_ref[...]


def _norm_kernel(x_ref, mod_ref, wg_ref, bg_ref, h_ref, gt_ref):
    _emit_mixer_input(x_ref[...], mod_ref, wg_ref, bg_ref, h_ref, gt_ref)


def _gate_specs(d, index):
    return [pl.BlockSpec((d, LANES), index), pl.BlockSpec((1, LANES), index)]


def _prenorm(x2, mod_l, w_gate, b_gate, seq):
    t, d = x2.shape
    bm = _tile(seq, 512)
    per_b = seq // bm
    row = lambda w: pl.BlockSpec((bm, w), lambda i: (i, 0))
    return pl.pallas_call(
        _norm_kernel,
        grid=(t // bm,),
        in_specs=[row(d), pl.BlockSpec((1, 6, d), lambda i: (i // per_b, 0, 0))] + _gate_specs(d, lambda i: (0, 0)),
        out_specs=[row(d), row(LANES)],
        out_shape=[jax.ShapeDtypeStruct((t, d), BF16), jax.ShapeDtypeStruct((t, LANES), F32)],
        compiler_params=_params(("parallel",), 32),
        name="prenorm",
    )(x2, mod_l, w_gate, b_gate)


def _mm_kernel(a_ref, w_ref, b_ref, o_ref):
    acc = jnp.dot(a_ref[...], w_ref[...], preferred_element_type=F32)
    o_ref[...] = (acc + b_ref[...]).astype(o_ref.dtype)


def _matmul_bias(a, w, bias, out_dtype, bm_pref=1024, bn_pref=1024):
    m, k = a.shape
    n = w.shape[1]
    bm = _tile(m, bm_pref)
    bn = _tile(n, bn_pref)
    return pl.pallas_call(
        _mm_kernel,
        grid=(n // bn, m // bm),
        in_specs=[pl.BlockSpec((bm, k), lambda j, i: (i, 0)),
                  pl.BlockSpec((k, bn), lambda j, i: (0, j)),
                  pl.BlockSpec((1, bn), lambda j, i: (0, j))],
        out_specs=pl.BlockSpec((bm, bn), lambda j, i: (i, j)),
        out_shape=jax.ShapeDtypeStruct((m, n), out_dtype),
        compiler_params=_params(("parallel", "arbitrary"), 48),
        name="matmul_bias",
    )(a, w, bias)


CONV_ROWS = 32
CONV_COLS = 256


def _fill_halo(xs_ref, prev, cur, nxt, first, last):
    hb = prev.shape[0]
    bm = cur.shape[0]
    xs_ref[0:hb, :] = jnp.where(first, 0.0, prev)
    xs_ref[hb:hb + bm, :] = cur
    xs_ref[hb + bm:hb + bm + hb, :] = jnp.where(last, 0.0, nxt)


def _conv_rows(xs_ref, w_ref, emit, *, taps, bm, width):
    pad = taps // 2
    lead = BF16_TILE_ROWS - pad
    span = CONV_ROWS + 2 * BF16_TILE_ROWS
    cb = min(CONV_COLS, width)
    assert bm % CONV_ROWS == 0 and width % cb == 0 and lead >= 0

    def body(rb, carry):
        r0 = pl.multiple_of(rb * CONV_ROWS, CONV_ROWS)
        for c0 in range(0, width, cb):
            blk = xs_ref[pl.ds(r0, span), c0:c0 + cb]
            acc = jnp.zeros((CONV_ROWS, cb), F32)
            for k in range(taps):
                acc = acc + w_ref[k:k + 1, c0:c0 + cb] * blk[lead + k:lead + k + CONV_ROWS, :]
            emit(r0, c0, acc)
        return carry

    lax.fori_loop(0, bm // CONV_ROWS, body, 0)


def _qkconv_kernel(prev_ref, cur_ref, next_ref, w_ref, o_ref, xs_ref, *, taps):
    i = pl.program_id(2)
    bm, width = cur_ref.shape[1], cur_ref.shape[2]
    _fill_halo(xs_ref, prev_ref[0].astype(F32), cur_ref[0].astype(F32), next_ref[0].astype(F32),
               i == 0, i == pl.num_programs(2) - 1)

    def emit(r0, c0, acc):
        o_ref[0, pl.ds(r0, CONV_ROWS), c0:c0 + acc.shape[1]] = acc.astype(o_ref.dtype)

    _conv_rows(xs_ref, w_ref, emit, taps=taps, bm=bm, width=width)


def _halo_specs(bm, cb, seq, col0):
    hb = BF16_TILE_ROWS
    per = bm // hb
    n_hb = seq // hb
    return [pl.BlockSpec((1, hb, cb), lambda b, j, i: (b, jnp.maximum(i * per - 1, 0), col0 + j)),
            pl.BlockSpec((1, bm, cb), lambda b, j, i: (b, i, col0 + j)),
            pl.BlockSpec((1, hb, cb), lambda b, j, i: (b, jnp.minimum((i + 1) * per, n_hb - 1), col0 + j))]


def _qk_conv(z3, w_qk, width):
    bsz, seq, _ = z3.shape
    taps = w_qk.shape[0]
    bm = _tile(seq, 512)
    cb = _tile(width, 2048)
    return pl.pallas_call(
        functools.partial(_qkconv_kernel, taps=taps),
        grid=(bsz, width // cb, seq // bm),
        in_specs=_halo_specs(bm, cb, seq, 0) + [pl.BlockSpec((taps, cb), lambda b, j, i: (0, j))],
        out_specs=pl.BlockSpec((1, bm, cb), lambda b, j, i: (b, i, j)),
        out_shape=jax.ShapeDtypeStruct((bsz, seq, width), BF16),
        scratch_shapes=[pltpu.VMEM((bm + 2 * BF16_TILE_ROWS, cb), F32)],
        compiler_params=_params(("parallel", "parallel", "arbitrary"), 32),
        name="qk_conv",
    )(z3, z3, z3, w_qk)


CONV_TOKENS = 8


def _glu_conv_kernel(prev_ref, cur_ref, next_ref, w_ref, b_ref, o_ref, u_ref, *, taps):
    i = pl.program_id(1)
    bm = cur_ref.shape[1]
    g = cur_ref.shape[2] // 2
    hb = BF16_TILE_ROWS
    lead = hb - taps // 2
    assert bm % CONV_TOKENS == 0 and lead >= 0

    def glu(blk):
        x = blk.astype(F32)
        return x[:, :g, :] * _sigmoid(x[:, g:, :])

    u_ref[0:hb] = jnp.where(i == 0, 0.0, glu(prev_ref[0]))
    u_ref[hb:hb + bm] = glu(cur_ref[0])
    u_ref[hb + bm:hb + bm + hb] = jnp.where(i == pl.num_programs(1) - 1, 0.0, glu(next_ref[0]))

    def body(grp, carry):
        t0 = grp * CONV_TOKENS
        acc = [b_ref[...]] * CONV_TOKENS
        for j in range(CONV_TOKENS + taps - 1):
            xin = u_ref[t0 + lead + j]
            for o in range(CONV_TOKENS):
                if 0 <= j - o < taps:
                    acc[o] = acc[o] + w_ref[j - o] * xin
        for o in range(CONV_TOKENS):
            o_ref[0, t0 + o] = acc[o]
        return carry

    lax.fori_loop(0, bm // CONV_TOKENS, body, 0)


def _glu_conv(glu4, w_dw, b_dw):
    bsz, seq, g2, _ = glu4.shape
    g = g2 // 2
    taps = w_dw.shape[0]
    bm = _tile(seq, 512)
    hb = BF16_TILE_ROWS
    per = bm // hb
    n_hb = seq // hb
    return pl.pallas_call(
        functools.partial(_glu_conv_kernel, taps=taps),
        grid=(bsz, seq // bm),
        in_specs=[pl.BlockSpec((1, hb, g2, LANES), lambda b, i: (b, jnp.maximum(i * per - 1, 0), 0, 0)),
                  pl.BlockSpec((1, bm, g2, LANES), lambda b, i: (b, i, 0, 0)),
                  pl.BlockSpec((1, hb, g2, LANES), lambda b, i: (b, jnp.minimum((i + 1) * per, n_hb - 1), 0, 0)),
                  pl.BlockSpec((taps, g, LANES), lambda b, i: (0, 0, 0)),
                  pl.BlockSpec((g, LANES), lambda b, i: (0, 0))],
        out_specs=pl.BlockSpec((1, bm, g, LANES), lambda b, i: (b, i, 0, 0)),
        out_shape=jax.ShapeDtypeStruct((bsz, seq, g, LANES), F32),
        scratch_shapes=[pltpu.VMEM((bm + 2 * hb, g, LANES), F32)],
        compiler_params=_params(("parallel", "arbitrary"), 32),
        name="glu_conv",
    )(glu4, glu4, glu4, w_dw.reshape(taps, g, LANES), b_dw.reshape(g, LANES))


CHUNKS_PER_STEP = 4
FWD, BWD = 0, 1


def _split2(x):
    hi = x.astype(BF16).astype(F32)
    lo = (x - hi).astype(BF16).astype(F32)
    return jnp.concatenate([hi, lo], axis=1)


def _column_forms(eye2, piece_rows):
    rhs = jnp.concatenate([jnp.broadcast_to(p, (LANES, p.shape[1])) for p in piece_rows], axis=0)
    return lax.dot_general(eye2, rhs.astype(BF16), (((1,), (1,)), ((), ())), preferred_element_type=F32)


def _mlstm_direction(q, k, v, a_row, a_col, i_col, b_tot, m_prev, m_new, c_ref, n_ref, bias):
    reps = q.shape[1] // LANES
    wide = lambda x: jnp.concatenate([x] * reps, axis=1)
    b_col = a_col + i_col
    dmat = (b_col - a_row) + bias
    inter = b_col + m_prev
    m_t = jnp.maximum(inter, jnp.max(dmat, axis=-1, keepdims=True))
    w_intra = jnp.exp(dmat - m_t)
    w_inter = jnp.exp(inter - m_t)
    s = lax.dot_general(q, k, (((1,), (1,)), ((), ())), preferred_element_type=F32) * w_intra
    c_prev = c_ref[...]
    n_prev = n_ref[...]
    num = jnp.dot(s.astype(BF16), v, preferred_element_type=F32) + \
        wide(w_inter) * jnp.dot(q, c_prev.astype(BF16), preferred_element_type=F32)
    den = jnp.sum(s, axis=-1, keepdims=True) + \
        w_inter * jnp.sum(q.astype(F32) * n_prev, axis=-1, keepdims=True)
    h = num * wide(1.0 / jnp.maximum(jnp.abs(den), jnp.exp(-m_t)))
    w_s = jnp.exp(b_tot - m_new - a_col)
    decay = wide(jnp.exp(b_tot + m_prev - m_new))
    kw = k.astype(F32) * wide(w_s)
    c_ref[...] = decay * c_prev + lax.dot_general(kw.astype(BF16), v, (((0,), (0,)), ((), ())),
                                                  preferred_element_type=F32)
    n_ref[...] = decay * n_prev + jnp.sum(kw, axis=0, keepdims=True)
    return h


def _mlstm_kernel(q_ref, k_ref, v_ref, o_ref, grow_ref, g_ref, out_ref,
                  hs_ref, a_ref, a2_ref, i2_ref, btot_ref, peak_ref, mprev_ref, mnew_ref,
                  eye_ref, bias_ref, c_ref, n_ref):
    length = CHUNK
    seq = q_ref.shape[1]
    n_chunks = seq // length
    row = lax.broadcasted_iota(I32, (length, length), 0)
    col = lax.broadcasted_iota(I32, (length, length), 1)
    lower = row >= col
    upper = row <= col
    lower_f = jnp.where(lower, 1.0, 0.0)
    upper_f = jnp.where(upper, 1.0, 0.0)
    eye_f = jnp.where(row == col, 1.0, 0.0)
    eye_ref[...] = jnp.concatenate([eye_f, eye_f], axis=1).astype(BF16)
    bias_ref[FWD] = jnp.where(lower, 0.0, NEG_INF)
    bias_ref[BWD] = jnp.where(upper, 0.0, NEG_INF)
    c_ref[...] = jnp.zeros(c_ref.shape, F32)
    n_ref[...] = jnp.zeros(n_ref.shape, F32)

    for d, cum_row, last in ((FWD, upper_f, length - 1), (BWD, lower_f, 0)):
        gate_i = grow_ref[0, 0, 2 * d]
        lf = _log_sigmoid(grow_ref[0, 0, 2 * d + 1])
        b_row = jnp.dot(lf, cum_row, preferred_element_type=F32, precision=HIGHEST)
        a = b_row - gate_i
        a_ref[d] = a
        a2_ref[d] = _split2(a)
        i2_ref[d] = _split2(gate_i)
        btot_ref[d] = jnp.broadcast_to(b_row[:, last:last + 1], (n_chunks, LANES))
        peak_ref[d] = jnp.broadcast_to(jnp.max(-a, axis=-1, keepdims=True), (n_chunks, LANES))

    def stabiliser_scan(c, carry):
        new = []
        for d, m in zip((FWD, BWD), carry):
            i = c if d == FWD else n_chunks - 1 - c
            mprev_ref[d, pl.ds(i, 1), :] = m
            m = btot_ref[d, pl.ds(i, 1), :] + jnp.maximum(m, peak_ref[d, pl.ds(i, 1), :])
            mnew_ref[d, pl.ds(i, 1), :] = m
            new.append(m)
        return tuple(new)

    zero = jnp.zeros((1, LANES), F32)
    lax.fori_loop(0, n_chunks, stabiliser_scan, (zero, zero))

    def run(d, i, a_col, i_col):
        r0 = pl.multiple_of(i * length, length)
        one = pl.ds(i, 1)
        h = _mlstm_direction(q_ref[0, pl.ds(r0, length), :], k_ref[0, pl.ds(r0, length), :],
                             v_ref[0, pl.ds(r0, length), :], a_ref[d, one, :], a_col, i_col,
                             btot_ref[d, one, :], mprev_ref[d, one, :], mnew_ref[d, one, :],
                             c_ref.at[d], n_ref.at[d], bias_ref[d])
        return r0, h

    def chunk_results(step):
        work = []
        for u in range(CHUNKS_PER_STEP):
            c = step * CHUNKS_PER_STEP + u
            work += [(FWD, c), (BWD, n_chunks - 1 - c)]
        rows = []
        for d, i in work:
            rows += [a2_ref[d, pl.ds(i, 1), :], i2_ref[d, pl.ds(i, 1), :]]
        cols = _column_forms(eye_ref[...], rows)
        lanes = lambda j: cols[:, j * LANES:(j + 1) * LANES]
        return [run(d, i, lanes(2 * j), lanes(2 * j + 1)) for j, (d, i) in enumerate(work)]

    def first_touch(step, carry):
        for r0, h in chunk_results(step):
            hs_ref[pl.ds(r0, length), :] = h
        return carry

    def second_touch(step, carry):
        for r0, h in chunk_results(step):
            hs = hs_ref[pl.ds(r0, length), :] + h
            gate = _sigmoid(o_ref[0, pl.ds(r0, length), :].astype(F32))
            out_ref[0, pl.ds(r0, length), :] = ((_rms(hs) * g_ref[...]) * gate).astype(out_ref.dtype)
        return carry

    steps = n_chunks // CHUNKS_PER_STEP
    lax.fori_loop(0, steps // 2, first_touch, 0)
    lax.fori_loop(steps // 2, steps, second_touch, 0)


def _mlstm_branch(qk3, z3, gates, m_norm_g, heads):
    bsz, seq, _ = z3.shape
    dh = HEAD_DIM
    wm = heads * dh
    n_chunks = seq // CHUNK
    assert CHUNK == LANES and dh % LANES == 0
    assert seq % CHUNK == 0 and n_chunks % (2 * CHUNKS_PER_STEP) == 0
    grow = gates.reshape(bsz, seq, 2, 2, heads).transpose(0, 4, 2, 3, 1).reshape(bsz, heads, 4, n_chunks, CHUNK)
    seq_blk = lambda col0: pl.BlockSpec((1, seq, dh), lambda b, h: (b, 0, col0 + h))
    per_chunk = lambda w: pltpu.VMEM((2, n_chunks, w), F32)
    return pl.pallas_call(
        _mlstm_kernel,
        grid=(bsz, heads),
        in_specs=[seq_blk(0), seq_blk(heads), seq_blk(2 * heads), seq_blk(3 * heads),
                  pl.BlockSpec((1, 1, 4, n_chunks, CHUNK), lambda b, h: (b, h, 0, 0, 0)),
                  pl.BlockSpec((1, dh), lambda b, h: (0, h))],
        out_specs=pl.BlockSpec((1, seq, dh), lambda b, h: (b, 0, h)),
        out_shape=jax.ShapeDtypeStruct((bsz, seq, wm), BF16),
        scratch_shapes=[pltpu.VMEM((seq, dh), F32),
                        per_chunk(CHUNK), per_chunk(2 * CHUNK), per_chunk(2 * CHUNK),
                        per_chunk(LANES), per_chunk(LANES), per_chunk(LANES), per_chunk(LANES),
                        pltpu.VMEM((CHUNK, 2 * CHUNK), BF16), pltpu.VMEM((2, CHUNK, CHUNK), F32),
                        pltpu.VMEM((2, dh, dh), F32), pltpu.VMEM((2, 1, dh), F32)],
        compiler_params=_params(("parallel", "parallel"), 58),
        name="mlstm",
    )(qk3, qk3, z3, z3, grow, m_norm_g.reshape(1, wm))


def _merge_kernel(hm_ref, uc_ref, ln_ref, gm_ref, gc_ref, x_ref, mod_ref, wm_ref, wc_ref, wo_ref, *rest, route):
    if route:
        rt_ref, xn_ref, hp_ref, lg_ref = rest
    else:
        xn_ref, hp_ref = rest
    yc = uc_ref[...]
    yc = yc - jnp.mean(yc, axis=-1, keepdims=True)
    var = jnp.mean(yc * yc, axis=-1, keepdims=True)
    u = _silu(yc * lax.rsqrt(var + EPS) * ln_ref[0:1, :] + ln_ref[1:2, :])
    y_m = jnp.dot(hm_ref[...], wm_ref[...], preferred_element_type=F32)
    y_c = jnp.dot(u.astype(BF16), wc_ref[...], preferred_element_type=F32)
    merged = _sigmoid(gm_ref[...].astype(F32)) * y_m + _sigmoid(gc_ref[...].astype(F32)) * y_c
    out = jnp.dot(merged.astype(BF16), wo_ref[...], preferred_element_type=F32)
    x_new = x_ref[...] + mod_ref[0, 2:3, :] * out
    xn_ref[...] = x_new
    h = _rms(x_new) * (1.0 + mod_ref[0, 4:5, :]) + mod_ref[0, 3:4, :]
    hp_ref[...] = _pack_halves(h)
    if route:
        h_hi = h.astype(BF16)
        h_lo = (h - h_hi.astype(F32)).astype(BF16)
        lg_ref[...] = (jnp.dot(h_hi, rt_ref[0], preferred_element_type=F32)
                       + jnp.dot(h_lo, rt_ref[0], preferred_element_type=F32)
                       + jnp.dot(h_hi, rt_ref[1], preferred_element_type=F32))


def _merge(hm, uc, ln_gb, z2, col_gm, x2, mod_l, w_m, w_c, w_o, seq, router_t):
    t, d = x2.shape
    wm_, wc_ = hm.shape[1], uc.shape[1]
    bm = _tile(seq, 256)
    per_b = seq // bm
    assert col_gm % d == 0
    blk = col_gm // d
    route = router_t is not None
    once = pl.Buffered(1)
    in_specs = [pl.BlockSpec((bm, wm_), lambda i: (i, 0)),
                pl.BlockSpec((bm, wc_), lambda i: (i, 0)),
                pl.BlockSpec((2, wc_), lambda i: (0, 0)),
                pl.BlockSpec((bm, d), lambda i: (i, blk)),
                pl.BlockSpec((bm, d), lambda i: (i, blk + 1)),
                pl.BlockSpec((bm, d), lambda i: (i, 0)),
                pl.BlockSpec((1, 6, d), lambda i: (i // per_b, 0, 0)),
                pl.BlockSpec((wm_, d), lambda i: (0, 0), pipeline_mode=once),
                pl.BlockSpec((wc_, d), lambda i: (0, 0), pipeline_mode=once),
                pl.BlockSpec((d, d), lambda i: (0, 0), pipeline_mode=once)]
    out_specs = [pl.BlockSpec((bm, d), lambda i: (i, 0)), pl.BlockSpec((bm, d // 2), lambda i: (i, 0))]
    out_shape = [jax.ShapeDtypeStruct((t, d), F32), jax.ShapeDtypeStruct((t, d // 2), U32)]
    args = [hm, uc, ln_gb, z2, z2, x2, mod_l, w_m, w_c, w_o]
    if route:
        in_specs.append(pl.BlockSpec((2, d, LANES), lambda i: (0, 0, 0)))
        out_specs.append(pl.BlockSpec((bm, LANES), lambda i: (i, 0)))
        out_shape.append(jax.ShapeDtypeStruct((t, LANES), F32))
        args.append(router_t)
    return pl.pallas_call(
        functools.partial(_merge_kernel, route=route),
        grid=(t // bm,),
        in_specs=in_specs, out_specs=out_specs, out_shape=out_shape,
        compiler_params=_params(("parallel",), 48),
        name="merge",
    )(*args)


FFN_ROWS = 512
DENSE_ROWS = 1024
FIRST_OF_GROUP = 1
VALID = 2


CAST_ROWS = 512


def _stage_group_weights(te_ref, tn_ref, w_hbm, stage_ref, wb_ref, sem):
    n = pl.program_id(0)
    r = pl.program_id(1)
    k_rows, bn = stage_ref.shape[1], stage_ref.shape[2]
    half = w_hbm.shape[2] // 2

    def fetch(e, col_block):
        c0 = col_block * bn
        return [pltpu.make_async_copy(w_hbm.at[e, :, pl.ds(pl.multiple_of(c0 + k * half, LANES), bn)],
                                      stage_ref.at[k], sem.at[k]) for k in range(2)]

    @pl.when((n == 0) & (r == 0))
    def _():
        for cp in fetch(te_ref[0], 0):
            cp.start()

    for cp in fetch(te_ref[r], n):
        cp.wait()

    step_rows = math.gcd(k_rows, CAST_ROWS)

    def cast(i, carry):
        rows = pl.ds(pl.multiple_of(i * step_rows, step_rows), step_rows)
        wb_ref[:, rows, :] = stage_ref[:, rows, :].astype(BF16)
        return carry

    lax.fori_loop(0, k_rows // step_rows, cast, 0)
    nxt = tn_ref[r]

    @pl.when(nxt >= 0)
    def _():
        for cp in fetch(nxt, n):
            cp.start()

    @pl.when((nxt < 0) & (n + 1 < pl.num_programs(0)))
    def _():
        for cp in fetch(te_ref[0], n + 1):
            cp.start()


def _group_scratch(k_rows, bn):
    assert math.gcd(k_rows, CAST_ROWS) % BF16_TILE_ROWS == 0
    return [pltpu.VMEM((2, k_rows, bn), F32), pltpu.VMEM((2, k_rows, bn), BF16), pltpu.SemaphoreType.DMA((2,))]


def _ffn_up_kernel(te_ref, tf_ref, tn_ref, x_ref, w_hbm, o_ref, stage_ref, wb_ref, sem):
    flag = tf_ref[pl.program_id(1)]

    @pl.when((flag & FIRST_OF_GROUP) != 0)
    def _():
        _stage_group_weights(te_ref, tn_ref, w_hbm, stage_ref, wb_ref, sem)

    @pl.when((flag & VALID) != 0)
    def _():
        lo, hi = _unpack_halves(x_ref[...])
        x = jnp.concatenate([lo.astype(BF16), hi.astype(BF16)], axis=1)
        a = jnp.dot(x, wb_ref[0], preferred_element_type=F32)
        g = jnp.dot(x, wb_ref[1], preferred_element_type=F32)
        o_ref[...] = (_silu(a) * g).astype(o_ref.dtype)

    @pl.when((flag & VALID) == 0)
    def _():
        o_ref[...] = jnp.zeros(o_ref.shape, o_ref.dtype)


def _ffn_up(xp, w13, tile_e, tile_f, tile_next, bm):
    rows, dh = xp.shape
    n_e, d, f2 = w13.shape
    f = f2 // 2
    bn = _tile(f, 1024)
    return pl.pallas_call(
        _ffn_up_kernel,
        grid_spec=pltpu.PrefetchScalarGridSpec(
            num_scalar_prefetch=3,
            grid=(f // bn, rows // bm),
            in_specs=[pl.BlockSpec((bm, dh), lambda j, r, te, tf, tn: (r, 0)),
                      pl.BlockSpec(memory_space=pl.ANY)],
            out_specs=pl.BlockSpec((bm, bn), lambda j, r, te, tf, tn: (r, j)),
            scratch_shapes=_group_scratch(d, bn)),
        out_shape=jax.ShapeDtypeStruct((rows, f), BF16),
        compiler_params=_params(("arbitrary", "arbitrary"), 48),
        name="ffn_up",
    )(tile_e, tile_f, tile_next, xp, w13)


def _ffn_down_kernel(te_ref, tf_ref, tn_ref, h_ref, w_hbm, o_ref, stage_ref, wb_ref, sem):
    flag = tf_ref[pl.program_id(1)]

    @pl.when((flag & FIRST_OF_GROUP) != 0)
    def _():
        _stage_group_weights(te_ref, tn_ref, w_hbm, stage_ref, wb_ref, sem)

    @pl.when((flag & VALID) != 0)
    def _():
        h = h_ref[...]
        ya = jnp.dot(h, wb_ref[0], preferred_element_type=F32)
        yb = jnp.dot(h, wb_ref[1], preferred_element_type=F32)
        o_ref[...] = _pack_halves(jnp.concatenate([ya, yb], axis=1))

    @pl.when((flag & VALID) == 0)
    def _():
        o_ref[...] = jnp.zeros(o_ref.shape, o_ref.dtype)


def _ffn_down(h, w2, tile_e, tile_f, tile_next, bm):
    rows, f = h.shape
    n_e, _, d = w2.shape
    dh = d // 2
    bn = _tile(dh, 256)
    nn = dh // bn
    return pl.pallas_call(
        _ffn_down_kernel,
        grid_spec=pltpu.PrefetchScalarGridSpec(
            num_scalar_prefetch=3,
            grid=(nn, rows // bm),
            in_specs=[pl.BlockSpec((bm, f), lambda n, r, te, tf, tn: (r, 0)),
                      pl.BlockSpec(memory_space=pl.ANY)],
            out_specs=pl.BlockSpec((bm, bn), lambda n, r, te, tf, tn: (r, n)),
            scratch_shapes=_group_scratch(f, bn)),
        out_shape=jax.ShapeDtypeStruct((rows, dh), U32),
        compiler_params=_params(("arbitrary", "arbitrary"), 48),
        name="ffn_down",
    )(tile_e, tile_f, tile_next, h, w2)


def _residual_epilogue(x, gate, f, nxt_refs, out_refs, final):
    nmod_ref, wg_ref, bg_ref, fin_ref = nxt_refs
    x_new = x + gate * f
    if final:
        out_refs[0][...] = _rms(x_new) * fin_ref[...]
    else:
        out_refs[0][...] = x_new
        _emit_mixer_input(x_new, nmod_ref, wg_ref, bg_ref, out_refs[1], out_refs[2])


def _residual_out(t, d, row, final):
    if final:
        return [row(d)], [jax.ShapeDtypeStruct((t, d), F32)]
    return ([row(d), row(d), row(LANES)],
            [jax.ShapeDtypeStruct((t, d), F32), jax.ShapeDtypeStruct((t, d), BF16),
             jax.ShapeDtypeStruct((t, LANES), F32)])


def _dense_residual_kernel(y_ref, x_ref, mod_ref, nmod_ref, wg_ref, bg_ref, fin_ref, *out_refs, final):
    lo, hi = _unpack_halves(y_ref[...])
    f = jnp.concatenate([lo, hi], axis=1)
    _residual_epilogue(x_ref[...], mod_ref[0, 5:6, :], f, (nmod_ref, wg_ref, bg_ref, fin_ref), out_refs, final)


def _dense_residual(yp, x2, mod_l, nxt, seq, final):
    t, d = x2.shape
    bm = _tile(seq, 512)
    per_b = seq // bm
    mod_spec = pl.BlockSpec((1, 6, d), lambda i: (i // per_b, 0, 0))
    row = lambda w: pl.BlockSpec((bm, w), lambda i: (i, 0))
    out_specs, out_shape = _residual_out(t, d, row, final)
    return pl.pallas_call(
        functools.partial(_dense_residual_kernel, final=final),
        grid=(t // bm,),
        in_specs=[row(d // 2), row(d), mod_spec, mod_spec] + _gate_specs(d, lambda i: (0, 0))
        + [pl.BlockSpec((1, d), lambda i: (0, 0))],
        out_specs=out_specs, out_shape=out_shape,
        compiler_params=_params(("parallel",), 40),
        name="dense_residual",
    )(yp, x2, mod_l, *nxt)


ROUTE_TOKENS = 512


def _route_kernel(lg_ref, idx_ref, wt_ref, cnt_ref, run_ref):
    step = pl.program_id(0)
    n_e, tb = lg_ref.shape

    @pl.when(step == 0)
    def _():
        run_ref[...] = jnp.zeros(run_ref.shape, F32)

    lg = lg_ref[...]
    eid = lax.broadcasted_iota(I32, (n_e, tb), 0)
    m1 = jnp.max(lg, axis=0, keepdims=True)
    e1 = jnp.min(jnp.where(lg == m1, eid, n_e), axis=0, keepdims=True)
    rest = jnp.where(eid == e1, NEG_INF, lg)
    m2 = jnp.max(rest, axis=0, keepdims=True)
    e2 = jnp.min(jnp.where(rest == m2, eid, n_e), axis=0, keepdims=True)
    p2 = jnp.exp(m2 - m1)
    w1 = 1.0 / (1.0 + p2)
    w2 = p2 / (1.0 + p2)
    member = jnp.where((eid == e1) | (eid == e2), 1.0, 0.0)
    before = (lax.broadcasted_iota(I32, (tb, tb), 0) < lax.broadcasted_iota(I32, (tb, tb), 1)).astype(BF16)
    rank = jnp.dot(member.astype(BF16), before, preferred_element_type=F32) + run_ref[:, 0:1]
    r1 = jnp.sum(jnp.where(eid == e1, rank, 0.0), axis=0, keepdims=True)
    r2 = jnp.sum(jnp.where(eid == e2, rank, 0.0), axis=0, keepdims=True)
    zero_i = jnp.zeros((SUBLANES - 4, tb), I32)
    idx_ref[...] = jnp.concatenate([e1, e2, r1.astype(I32), r2.astype(I32), zero_i], axis=0)
    wt_ref[...] = jnp.concatenate([w1, w2, jnp.zeros((SUBLANES - 2, tb), F32)], axis=0)
    run_ref[...] = run_ref[...] + jnp.sum(member, axis=1, keepdims=True)
    cnt_ref[...] = run_ref[...]


def _route(logits_t):
    n_e, t = logits_t.shape
    assert n_e % SUBLANES == 0
    tb = _tile(t, ROUTE_TOKENS)
    return pl.pallas_call(
        _route_kernel,
        grid=(t // tb,),
        in_specs=[pl.BlockSpec((n_e, tb), lambda i: (0, i))],
        out_specs=[pl.BlockSpec((SUBLANES, tb), lambda i: (0, i)),
                   pl.BlockSpec((SUBLANES, tb), lambda i: (0, i)),
                   pl.BlockSpec((n_e, LANES), lambda i: (0, 0))],
        out_shape=[jax.ShapeDtypeStruct((SUBLANES, t), I32), jax.ShapeDtypeStruct((SUBLANES, t), F32),
                   jax.ShapeDtypeStruct((n_e, LANES), F32)],
        scratch_shapes=[pltpu.VMEM((n_e, LANES), F32)],
        compiler_params=_params(("arbitrary",), 32),
        name="route",
    )(logits_t)


MOVE_TOKENS = 256
ISSUE_UNROLL = 8


def _row_copy(src, dst, sem):
    return pltpu.make_async_copy(src, dst, sem)


def _dispatch_kernel(dest_ref, x_ref, init_ref, out_ref, sem):
    del init_ref
    n_tok = x_ref.shape[0]
    total = dest_ref.shape[0] // TOP_K
    base = pl.program_id(0) * n_tok

    def issue(i, carry):
        for k in range(TOP_K):
            _row_copy(x_ref.at[pl.ds(i, 1)], out_ref.at[pl.ds(dest_ref[k * total + base + i], 1)],
                      sem).start(priority=k % 2)
        return carry

    lax.fori_loop(0, n_tok, issue, 0, unroll=ISSUE_UNROLL)
    for k in range(TOP_K):
        _row_copy(x_ref, out_ref.at[pl.ds(0, n_tok)], sem).wait()


def _dispatch(xp, dest_flat, rows):
    t, dh = xp.shape
    nt = _tile(t, MOVE_TOKENS)
    return pl.pallas_call(
        _dispatch_kernel,
        grid_spec=pltpu.PrefetchScalarGridSpec(
            num_scalar_prefetch=1,
            grid=(t // nt,),
            in_specs=[pl.BlockSpec((nt, dh), lambda i, dest: (i, 0)),
                      pl.BlockSpec(memory_space=pl.ANY)],
            out_specs=pl.BlockSpec(memory_space=pl.ANY),
            scratch_shapes=[pltpu.SemaphoreType.DMA(())]),
        out_shape=jax.ShapeDtypeStruct((rows, dh), U32),
        input_output_aliases={2: 0},
        compiler_params=_params(("arbitrary",), 32),
        name="dispatch",
    )(dest_flat, xp, jnp.zeros((rows, dh), U32))


def _combine_kernel(dest_ref, y_ref, wt_ref, x_ref, mod_ref, nmod_ref, wg_ref, bg_ref, fin_ref, *rest, final):
    out_refs, (buf_ref, sem) = rest[:-2], rest[-2:]
    n_tok = x_ref.shape[0]
    total = dest_ref.shape[0] // TOP_K
    step = pl.program_id(0)

    def gather(s):
        slot = s % 2

        def issue(i, carry):
            for k in range(TOP_K):
                _row_copy(y_ref.at[pl.ds(dest_ref[k * total + s * n_tok + i], 1)],
                          buf_ref.at[slot, k, pl.ds(i, 1)], sem.at[slot]).start(priority=k % 2)
            return carry

        lax.fori_loop(0, n_tok, issue, 0, unroll=ISSUE_UNROLL)

    @pl.when(step == 0)
    def _():
        gather(step)

    @pl.when(step + 1 < pl.num_programs(0))
    def _():
        gather(step + 1)

    slot = step % 2
    for k in range(TOP_K):
        _row_copy(y_ref.at[pl.ds(0, n_tok)], buf_ref.at[slot, k], sem.at[slot]).wait()
    f = None
    for k in range(TOP_K):
        lo, hi = _unpack_halves(buf_ref[slot, k])
        part = wt_ref[:, k:k + 1] * jnp.concatenate([lo, hi], axis=1)
        f = part if f is None else f + part
    _residual_epilogue(x_ref[...], mod_ref[0, 5:6, :], f, (nmod_ref, wg_ref, bg_ref, fin_ref), out_refs, final)


def _combine(yp, dest_flat, wt_cols, x2, mod_l, nxt, seq, final):
    t, d = x2.shape
    dh = d // 2
    nt = _tile(seq, MOVE_TOKENS)
    per_b = seq // nt
    mod_spec = pl.BlockSpec((1, 6, d), lambda i, dest: (i // per_b, 0, 0))
    row = lambda w: pl.BlockSpec((nt, w), lambda i, dest: (i, 0))
    out_specs, out_shape = _residual_out(t, d, row, final)
    return pl.pallas_call(
        functools.partial(_combine_kernel, final=final),
        grid_spec=pltpu.PrefetchScalarGridSpec(
            num_scalar_prefetch=1,
            grid=(t // nt,),
            in_specs=[pl.BlockSpec(memory_space=pl.ANY), row(TOP_K), row(d), mod_spec, mod_spec]
            + _gate_specs(d, lambda i, dest: (0, 0)) + [pl.BlockSpec((1, d), lambda i, dest: (0, 0))],
            out_specs=out_specs,
            scratch_shapes=[pltpu.VMEM((2, TOP_K, nt, dh), U32), pltpu.SemaphoreType.DMA((2,))]),
        out_shape=out_shape,
        compiler_params=_params(("arbitrary",), 40),
        name="combine",
    )(dest_flat, yp, wt_cols, x2, mod_l, *nxt)


def _dense_ffn(hp, x2, w13, w2, mod_l, nxt, seq, final):
    t = hp.shape[0]
    bm = math.gcd(t, DENSE_ROWS)
    n_tiles = t // bm
    tile_e = jnp.zeros((n_tiles,), I32)
    tile_f = jnp.full((n_tiles,), VALID, I32).at[0].set(VALID | FIRST_OF_GROUP)
    tile_next = jnp.full((n_tiles,), -1, I32)
    hid = _ffn_up(hp, w13[None], tile_e, tile_f, tile_next, bm)
    yp = _ffn_down(hid, w2[None], tile_e, tile_f, tile_next, bm)
    return _dense_residual(yp, x2, mod_l, nxt, seq, final)


def _moe_ffn(hp, logits_t, x2, w13, w2, mod_l, nxt, seq, final):
    t = hp.shape[0]
    n_e = w13.shape[0]
    idx, wts, cnt = _route(logits_t)
    counts = cnt[:, 0].astype(I32)
    padded = ((counts + FFN_ROWS - 1) // FFN_ROWS) * FFN_ROWS
    seg_end = jnp.cumsum(padded)
    seg_start = seg_end - padded
    n_tiles = (t * TOP_K) // FFN_ROWS + n_e
    tile_row0 = jnp.arange(n_tiles, dtype=I32) * FFN_ROWS
    tile_e = jnp.minimum(jnp.searchsorted(seg_end, tile_row0, side="right"), n_e - 1).astype(I32)
    valid = tile_row0 < seg_end[-1]
    first = jnp.concatenate([jnp.ones((1,), bool), tile_e[1:] != tile_e[:-1]]) & valid
    tile_f = (jnp.where(valid, VALID, 0) | jnp.where(first, FIRST_OF_GROUP, 0)).astype(I32)
    tile_id = jnp.arange(n_tiles, dtype=I32)
    later_start = lax.cummin(jnp.where(first, tile_id, n_tiles), reverse=True)
    next_start = jnp.concatenate([later_start[1:], jnp.full((1,), n_tiles, I32)])
    tile_next = jnp.where(next_start < n_tiles, tile_e[jnp.minimum(next_start, n_tiles - 1)], -1).astype(I32)
    start_of = jnp.sum(jnp.where(idx[:TOP_K, :, None] == jnp.arange(n_e, dtype=I32), seg_start, 0), axis=-1)
    dest = (start_of + idx[TOP_K:2 * TOP_K]).reshape(-1).astype(I32)
    xs = _dispatch(hp, dest, n_tiles * FFN_ROWS)
    hid = _ffn_up(xs, w13, tile_e, tile_f, tile_next, FFN_ROWS)
    yp = _ffn_down(hid, w2, tile_e, tile_f, tile_next, FFN_ROWS)
    return _combine(yp, dest, wts[:TOP_K].T, x2, mod_l, nxt, seq, final)


def kernel(x, c, w_mod, b_mod, w_in, b_in, w_qk_conv, m_norm_g, w_m_proj, w_dw, b_dw, ln_c_g, ln_c_b,
           w_c_proj, w_out, ffn_w13, ffn_w2, moe_router, moe_w13, moe_w2, final_g):
    bsz, seq, d = x.shape
    depth = w_mod.shape[0]
    wm = w_m_proj.shape[1]
    wc = w_dw.shape[2]
    heads = wm // HEAD_DIM
    n_gate = 4 * heads
    off_g = 4 * wm
    off_glu = off_g + n_gate
    t = bsz * seq
    assert d % (2 * LANES) == 0 and n_gate <= LANES

    mods = _adaln_mod(c, w_mod, b_mod)
    x2 = x.reshape(t, d)

    def gate_params(l):
        w_gate = jnp.zeros((d, LANES), BF16).at[:, :n_gate].set(w_in[l][:, off_g:off_glu].astype(BF16))
        b_gate = jnp.zeros((1, LANES), F32).at[0, :n_gate].set(b_in[l][off_g:off_glu])
        return w_gate, b_gate

    h, gates = _prenorm(x2, mods[0], *gate_params(0), seq)
    out = None
    for l in range(depth):
        final = l == depth - 1
        mod_l = mods[l]
        l_next = l if final else l + 1
        nxt = (mods[l_next], *gate_params(l_next), final_g.reshape(1, d))
        w_main = jnp.concatenate([w_in[l][:, :off_g], w_in[l][:, off_glu:]], axis=1).astype(BF16)
        b_main = jnp.concatenate([b_in[l][:off_g], b_in[l][off_glu:]])[None]
        z = _matmul_bias(h, w_main, b_main, BF16)
        gates = gates[:, :n_gate]
        z3 = z.reshape(bsz, seq, z.shape[1])
        k_scale = jnp.concatenate([jnp.ones((wm,), F32), jnp.full((wm,), HEAD_DIM ** -0.5, F32)])
        qk3 = _qk_conv(z3, w_qk_conv[l] * k_scale[None], 2 * wm)
        hm = _mlstm_branch(qk3, z3, gates.reshape(bsz, seq, n_gate), m_norm_g[l], heads)
        glu4 = z[:, off_g:off_g + 2 * wc].reshape(bsz, seq, 2 * wc // LANES, LANES)
        uc = _glu_conv(glu4, w_dw[l], b_dw[l])
        ln_gb = jnp.stack([ln_c_g[l], ln_c_b[l]])
        moe = l % 2 == 1
        router_t = None
        if moe:
            n_e = moe_router.shape[2]
            r_pad = jnp.zeros((d, LANES), F32).at[:, :n_e].set(moe_router[l // 2])
            r_hi = r_pad.astype(BF16)
            router_t = jnp.stack([r_hi, (r_pad - r_hi.astype(F32)).astype(BF16)])
        res = _merge(hm.reshape(t, wm), uc.reshape(t, wc), ln_gb, z, off_g + 2 * wc, x2, mod_l,
                     w_m_proj[l].astype(BF16), w_c_proj[l].astype(BF16), w_out[l].astype(BF16), seq, router_t)
        if moe:
            x2, hp, logits = res
            res = _moe_ffn(hp, logits[:, :n_e].T, x2, moe_w13[l // 2], moe_w2[l // 2], mod_l, nxt, seq, final)
        else:
            x2, hp = res
            res = _dense_ffn(hp, x2, ffn_w13[l // 2], ffn_w2[l // 2], mod_l, nxt, seq, final)
        if final:
            out = res[0]
        else:
            x2, h, gates = res
    return out.reshape(bsz, seq, d)
```

```python
import functools
import math

import jax
import jax.numpy as jnp
from jax import lax
from jax.experimental import pallas as pl
from jax.experimental.pallas import tpu as pltpu

F32 = jnp.float32
BF16 = jnp.bfloat16
U32 = jnp.uint32
I32 = jnp.int32

HEAD_DIM = 256
CHUNK = 128
TOP_K = 2
EPS = 1e-6

V7X_VMEM_BYTES = 64 * 1024 * 1024
LANES = 128
SUBLANES = 8
BF16_TILE_ROWS = 16

HIGHEST = lax.Precision.HIGHEST
NEG_INF = float("-inf")


def _params(semantics, vmem_mib):
    assert vmem_mib * 1024 * 1024 < V7X_VMEM_BYTES
    return pltpu.CompilerParams(dimension_semantics=semantics, vmem_limit_bytes=vmem_mib * 1024 * 1024)


def _tile(n, pref):
    if n <= pref:
        return n
    t = pref - pref % LANES
    while t > LANES and n % t:
        t -= LANES
    assert n % t == 0, (n, pref)
    return t


def _sigmoid(x):
    return 1.0 / (1.0 + jnp.exp(-x))


def _silu(x):
    return x * _sigmoid(x)


def _log_sigmoid(x):
    return jnp.minimum(x, 0.0) - jnp.log1p(jnp.exp(-jnp.abs(x)))


def _rms(x):
    return x * lax.rsqrt(jnp.mean(x * x, axis=-1, keepdims=True) + EPS)


def _pack_halves(y):
    n = y.shape[-1] // 2
    lo = lax.bitcast_convert_type(y[:, :n].astype(BF16).astype(F32), U32)
    hi = lax.bitcast_convert_type(y[:, n:].astype(BF16).astype(F32), U32)
    return (hi & jnp.uint32(0xFFFF0000)) | (lo >> 16)


def _unpack_halves(u):
    lo = lax.bitcast_convert_type(u << 16, F32)
    hi = lax.bitcast_convert_type(u & jnp.uint32(0xFFFF0000), F32)
    return lo, hi


def _mod_kernel(c_ref, w_ref, b_ref, o_ref):
    c = _silu(c_ref[...])
    w = w_ref[0]
    c_hi = c.astype(BF16)
    c_lo = (c - c_hi.astype(F32)).astype(BF16)
    w_hi = w.astype(BF16)
    w_lo = (w - w_hi.astype(F32)).astype(BF16)
    o_ref[0] = (jnp.dot(c_hi, w_hi, preferred_element_type=F32) + jnp.dot(c_lo, w_hi, preferred_element_type=F32)
                + jnp.dot(c_hi, w_lo, preferred_element_type=F32) + b_ref[0])


def _adaln_mod(c, w_mod, b_mod):
    depth, d, n = w_mod.shape
    b = c.shape[0]
    c_pad = jnp.zeros((SUBLANES, d), F32).at[:b].set(c)
    bn = _tile(n, 1024)
    out = pl.pallas_call(
        _mod_kernel,
        grid=(depth, n // bn),
        in_specs=[pl.BlockSpec((SUBLANES, d), lambda l, j: (0, 0)),
                  pl.BlockSpec((1, d, bn), lambda l, j: (l, 0, j)),
                  pl.BlockSpec((1, 1, bn), lambda l, j: (l, 0, j))],
        out_specs=pl.BlockSpec((1, SUBLANES, bn), lambda l, j: (l, 0, j)),
        out_shape=jax.ShapeDtypeStruct((depth, SUBLANES, n), F32),
        compiler_params=_params(("arbitrary", "arbitrary"), 40),
        name="adaln_mod",
    )(c_pad, w_mod, b_mod.reshape(depth, 1, n))
    return out[:, :b].reshape(depth, b, 6, d)


def _emit_mixer_input(x, mod_ref, wg_ref, bg_ref, h_ref, gt_ref):
    h = (_rms(x) * (1.0 + mod_ref[0, 1:2, :]) + mod_ref[0, 0:1, :]).astype(BF16)
    h_ref[...] = h
    gt_ref[...] = jnp.dot(h, wg_ref[...], preferred_element_type=F32) + bg_ref[...]


def _norm_kernel(x_ref, mod_ref, wg_ref, bg_ref, h_ref, gt_ref):
    _emit_mixer_input(x_ref[...], mod_ref, wg_ref, bg_ref, h_ref, gt_ref)


def _gate_specs(d, index):
    return [pl.BlockSpec((d, LANES), index), pl.BlockSpec((1, LANES), index)]


def _prenorm(x2, mod_l, w_gate, b_gate, seq):
    t, d = x2.shape
    bm = _tile(seq, 512)
    per_b = seq // bm
    row = lambda w: pl.BlockSpec((bm, w), lambda i: (i, 0))
    return pl.pallas_call(
        _norm_kernel,
        grid=(t // bm,),
        in_specs=[row(d), pl.BlockSpec((1, 6, d), lambda i: (i // per_b, 0, 0))] + _gate_specs(d, lambda i: (0, 0)),
        out_specs=[row(d), row(LANES)],
        out_shape=[jax.ShapeDtypeStruct((t, d), BF16), jax.ShapeDtypeStruct((t, LANES), F32)],
        compiler_params=_params(("parallel",), 32),
        name="prenorm",
    )(x2, mod_l, w_gate, b_gate)


def _mm_kernel(a_ref, w_ref, b_ref, o_ref):
    acc = jnp.dot(a_ref[...], w_ref[...], preferred_element_type=F32)
    o_ref[...] = (acc + b_ref[...]).astype(o_ref.dtype)


def _matmul_bias(a, w, bias, out_dtype, bm_pref=1024, bn_pref=1024):
    m, k = a.shape
    n = w.shape[1]
    bm = _tile(m, bm_pref)
    bn = _tile(n, bn_pref)
    return pl.pallas_call(
        _mm_kernel,
        grid=(n // bn, m // bm),
        in_specs=[pl.BlockSpec((bm, k), lambda j, i: (i, 0)),
                  pl.BlockSpec((k, bn), lambda j, i: (0, j)),
                  pl.BlockSpec((1, bn), lambda j, i: (0, j))],
        out_specs=pl.BlockSpec((bm, bn), lambda j, i: (i, j)),
        out_shape=jax.ShapeDtypeStruct((m, n), out_dtype),
        compiler_params=_params(("parallel", "arbitrary"), 48),
        name="matmul_bias",
    )(a, w, bias)


CONV_ROWS = 32
CONV_COLS = 256


def _fill_halo(xs_ref, prev, cur, nxt, first, last):
    hb = prev.shape[0]
    bm = cur.shape[0]
    xs_ref[0:hb, :] = jnp.where(first, 0.0, prev)
    xs_ref[hb:hb + bm, :] = cur
    xs_ref[hb + bm:hb + bm + hb, :] = jnp.where(last, 0.0, nxt)


def _conv_rows(xs_ref, w_ref, emit, *, taps, bm, width):
    pad = taps // 2
    lead = BF16_TILE_ROWS - pad
    span = CONV_ROWS + 2 * BF16_TILE_ROWS
    cb = min(CONV_COLS, width)
    assert bm % CONV_ROWS == 0 and width % cb == 0 and lead >= 0

    def body(rb, carry):
        r0 = pl.multiple_of(rb * CONV_ROWS, CONV_ROWS)
        for c0 in range(0, width, cb):
            blk = xs_ref[pl.ds(r0, span), c0:c0 + cb]
            acc = jnp.zeros((CONV_ROWS, cb), F32)
            for k in range(taps):
                acc = acc + w_ref[k:k + 1, c0:c0 + cb] * blk[lead + k:lead + k + CONV_ROWS, :]
            emit(r0, c0, acc)
        return carry

    lax.fori_loop(0, bm // CONV_ROWS, body, 0)


def _qkconv_kernel(prev_ref, cur_ref, next_ref, w_ref, o_ref, xs_ref, *, taps):
    i = pl.program_id(2)
    bm, width = cur_ref.shape[1], cur_ref.shape[2]
    _fill_halo(xs_ref, prev_ref[0].astype(F32), cur_ref[0].astype(F32), next_ref[0].astype(F32),
               i == 0, i == pl.num_programs(2) - 1)

    def emit(r0, c0, acc):
        o_ref[0, pl.ds(r0, CONV_ROWS), c0:c0 + acc.shape[1]] = acc.astype(o_ref.dtype)

    _conv_rows(xs_ref, w_ref, emit, taps=taps, bm=bm, width=width)


def _halo_specs(bm, cb, seq, col0):
    hb = BF16_TILE_ROWS
    per = bm // hb
    n_hb = seq // hb
    return [pl.BlockSpec((1, hb, cb), lambda b, j, i: (b, jnp.maximum(i * per - 1, 0), col0 + j)),
            pl.BlockSpec((1, bm, cb), lambda b, j, i: (b, i, col0 + j)),
            pl.BlockSpec((1, hb, cb), lambda b, j, i: (b, jnp.minimum((i + 1) * per, n_hb - 1), col0 + j))]


def _qk_conv(z3, w_qk, width):
    bsz, seq, _ = z3.shape
    taps = w_qk.shape[0]
    bm = _tile(seq, 512)
    cb = _tile(width, 2048)
    return pl.pallas_call(
        functools.partial(_qkconv_kernel, taps=taps),
        grid=(bsz, width // cb, seq // bm),
        in_specs=_halo_specs(bm, cb, seq, 0) + [pl.BlockSpec((taps, cb), lambda b, j, i: (0, j))],
        out_specs=pl.BlockSpec((1, bm, cb), lambda b, j, i: (b, i, j)),
        out_shape=jax.ShapeDtypeStruct((bsz, seq, width), BF16),
        scratch_shapes=[pltpu.VMEM((bm + 2 * BF16_TILE_ROWS, cb), F32)],
        compiler_params=_params(("parallel", "parallel", "arbitrary"), 32),
        name="qk_conv",
    )(z3, z3, z3, w_qk)


CONV_TOKENS = 8


def _glu_conv_kernel(prev_ref, cur_ref, next_ref, w_ref, b_ref, o_ref, u_ref, *, taps):
    i = pl.program_id(1)
    bm = cur_ref.shape[1]
    g = cur_ref.shape[2] // 2
    hb = BF16_TILE_ROWS
    lead = hb - taps // 2
    assert bm % CONV_TOKENS == 0 and lead >= 0

    def glu(blk):
        x = blk.astype(F32)
        return x[:, :g, :] * _sigmoid(x[:, g:, :])

    u_ref[0:hb] = jnp.where(i == 0, 0.0, glu(prev_ref[0]))
    u_ref[hb:hb + bm] = glu(cur_ref[0])
    u_ref[hb + bm:hb + bm + hb] = jnp.where(i == pl.num_programs(1) - 1, 0.0, glu(next_ref[0]))

    def body(grp, carry):
        t0 = grp * CONV_TOKENS
        acc = [b_ref[...]] * CONV_TOKENS
        for j in range(CONV_TOKENS + taps - 1):
            xin = u_ref[t0 + lead + j]
            for o in range(CONV_TOKENS):
                if 0 <= j - o < taps:
                    acc[o] = acc[o] + w_ref[j - o] * xin
        for o in range(CONV_TOKENS):
            o_ref[0, t0 + o] = acc[o]
        return carry

    lax.fori_loop(0, bm // CONV_TOKENS, body, 0)


def _glu_conv(glu4, w_dw, b_dw):
    bsz, seq, g2, _ = glu4.shape
    g = g2 // 2
    taps = w_dw.shape[0]
    bm = _tile(seq, 512)
    hb = BF16_TILE_ROWS
    per = bm // hb
    n_hb = seq // hb
    return pl.pallas_call(
        functools.partial(_glu_conv_kernel, taps=taps),
        grid=(bsz, seq // bm),
        in_specs=[pl.BlockSpec((1, hb, g2, LANES), lambda b, i: (b, jnp.maximum(i * per - 1, 0), 0, 0)),
                  pl.BlockSpec((1, bm, g2, LANES), lambda b, i: (b, i, 0, 0)),
                  pl.BlockSpec((1, hb, g2, LANES), lambda b, i: (b, jnp.minimum((i + 1) * per, n_hb - 1), 0, 0)),
                  pl.BlockSpec((taps, g, LANES), lambda b, i: (0, 0, 0)),
                  pl.BlockSpec((g, LANES), lambda b, i: (0, 0))],
        out_specs=pl.BlockSpec((1, bm, g, LANES), lambda b, i: (b, i, 0, 0)),
        out_shape=jax.ShapeDtypeStruct((bsz, seq, g, LANES), F32),
        scratch_shapes=[pltpu.VMEM((bm + 2 * hb, g, LANES), F32)],
        compiler_params=_params(("parallel", "arbitrary"), 32),
        name="glu_conv",
    )(glu4, glu4, glu4, w_dw.reshape(taps, g, LANES), b_dw.reshape(g, LANES))


CHUNKS_PER_STEP = 4
FWD, BWD = 0, 1


def _split2(x):
    hi = x.astype(BF16).astype(F32)
    lo = (x - hi).astype(BF16).astype(F32)
    return jnp.concatenate([hi, lo], axis=1)


def _column_forms(eye2, piece_rows):
    rhs = jnp.concatenate([jnp.broadcast_to(p, (LANES, p.shape[1])) for p in piece_rows], axis=0)
    return lax.dot_general(eye2, rhs.astype(BF16), (((1,), (1,)), ((), ())), preferred_element_type=F32)


def _mlstm_direction(q, k, v, a_row, a_col, i_col, b_tot, m_prev, m_new, c_ref, n_ref, bias):
    reps = q.shape[1] // LANES
    wide = lambda x: jnp.concatenate([x] * reps, axis=1)
    b_col = a_col + i_col
    dmat = (b_col - a_row) + bias
    inter = b_col + m_prev
    m_t = jnp.maximum(inter, jnp.max(dmat, axis=-1, keepdims=True))
    w_intra = jnp.exp(dmat - m_t)
    w_inter = jnp.exp(inter - m_t)
    s = lax.dot_general(q, k, (((1,), (1,)), ((), ())), preferred_element_type=F32) * w_intra
    c_prev = c_ref[...]
    n_prev = n_ref[...]
    num = jnp.dot(s.astype(BF16), v, preferred_element_type=F32) + \
        wide(w_inter) * jnp.dot(q, c_prev.astype(BF16), preferred_element_type=F32)
    den = jnp.sum(s, axis=-1, keepdims=True) + \
        w_inter * jnp.sum(q.astype(F32) * n_prev, axis=-1, keepdims=True)
    h = num * wide(1.0 / jnp.maximum(jnp.abs(den), jnp.exp(-m_t)))
    w_s = jnp.exp(b_tot - m_new - a_col)
    decay = wide(jnp.exp(b_tot + m_prev - m_new))
    kw = k.astype(F32) * wide(w_s)
    c_ref[...] = decay * c_prev + lax.dot_general(kw.astype(BF16), v, (((0,), (0,)), ((), ())),
                                                  preferred_element_type=F32)
    n_ref[...] = decay * n_prev + jnp.sum(kw, axis=0, keepdims=True)
    return h


def _mlstm_kernel(q_ref, k_ref, v_ref, o_ref, grow_ref, g_ref, out_ref,
                  hs_ref, a_ref, a2_ref, i2_ref, btot_ref, peak_ref, mprev_ref, mnew_ref,
                  eye_ref, bias_ref, c_ref, n_ref):
    length = CHUNK
    seq = q_ref.shape[1]
    n_chunks = seq // length
    row = lax.broadcasted_iota(I32, (length, length), 0)
    col = lax.broadcasted_iota(I32, (length, length), 1)
    lower = row >= col
    upper = row <= col
    lower_f = jnp.where(lower, 1.0, 0.0)
    upper_f = jnp.where(upper, 1.0, 0.0)
    eye_f = jnp.where(row == col, 1.0, 0.0)
    eye_ref[...] = jnp.concatenate([eye_f, eye_f], axis=1).astype(BF16)
    bias_ref[FWD] = jnp.where(lower, 0.0, NEG_INF)
    bias_ref[BWD] = jnp.where(upper, 0.0, NEG_INF)
    c_ref[...] = jnp.zeros(c_ref.shape, F32)
    n_ref[...] = jnp.zeros(n_ref.shape, F32)

    for d, cum_row, last in ((FWD, upper_f, length - 1), (BWD, lower_f, 0)):
        gate_i = grow_ref[0, 0, 2 * d]
        lf = _log_sigmoid(grow_ref[0, 0, 2 * d + 1])
        b_row = jnp.dot(lf, cum_row, preferred_element_type=F32, precision=HIGHEST)
        a = b_row - gate_i
        a_ref[d] = a
        a2_ref[d] = _split2(a)
        i2_ref[d] = _split2(gate_i)
        btot_ref[d] = jnp.broadcast_to(b_row[:, last:last + 1], (n_chunks, LANES))
        peak_ref[d] = jnp.broadcast_to(jnp.max(-a, axis=-1, keepdims=True), (n_chunks, LANES))

    def stabiliser_scan(c, carry):
        new = []
        for d, m in zip((FWD, BWD), carry):
            i = c if d == FWD else n_chunks - 1 - c
            mprev_ref[d, pl.ds(i, 1), :] = m
            m = btot_ref[d, pl.ds(i, 1), :] + jnp.maximum(m, peak_ref[d, pl.ds(i, 1), :])
            mnew_ref[d, pl.ds(i, 1), :] = m
            new.append(m)
        return tuple(new)

    zero = jnp.zeros((1, LANES), F32)
    lax.fori_loop(0, n_chunks, stabiliser_scan, (zero, zero))

    def run(d, i, a_col, i_col):
        r0 = pl.multiple_of(i * length, length)
        one = pl.ds(i, 1)
        h = _mlstm_direction(q_ref[0, pl.ds(r0, length), :], k_ref[0, pl.ds(r0, length), :],
                             v_ref[0, pl.ds(r0, length), :], a_ref[d, one, :], a_col, i_col,
                             btot_ref[d, one, :], mprev_ref[d, one, :], mnew_ref[d, one, :],
                             c_ref.at[d], n_ref.at[d], bias_ref[d])
        return r0, h

    def chunk_results(step):
        work = []
        for u in range(CHUNKS_PER_STEP):
            c = step * CHUNKS_PER_STEP + u
            work += [(FWD, c), (BWD, n_chunks - 1 - c)]
        rows = []
        for d, i in work:
            rows += [a2_ref[d, pl.ds(i, 1), :], i2_ref[d, pl.ds(i, 1), :]]
        cols = _column_forms(eye_ref[...], rows)
        lanes = lambda j: cols[:, j * LANES:(j + 1) * LANES]
        return [run(d, i, lanes(2 * j), lanes(2 * j + 1)) for j, (d, i) in enumerate(work)]

    def first_touch(step, carry):
        for r0, h in chunk_results(step):
            hs_ref[pl.ds(r0, length), :] = h
        return carry

    def second_touch(step, carry):
        for r0, h in chunk_results(step):
            hs = hs_ref[pl.ds(r0, length), :] + h
            gate = _sigmoid(o_ref[0, pl.ds(r0, length), :].astype(F32))
            out_ref[0, pl.ds(r0, length), :] = ((_rms(hs) * g_ref[...]) * gate).astype(out_ref.dtype)
        return carry

    steps = n_chunks // CHUNKS_PER_STEP
    lax.fori_loop(0, steps // 2, first_touch, 0)
    lax.fori_loop(steps // 2, steps, second_touch, 0)


def _mlstm_branch(qk3, z3, gates, m_norm_g, heads):
    bsz, seq, _ = z3.shape
    dh = HEAD_DIM
    wm = heads * dh
    n_chunks = seq // CHUNK
    assert CHUNK == LANES and dh % LANES == 0
    assert seq % CHUNK == 0 and n_chunks % (2 * CHUNKS_PER_STEP) == 0
    grow = gates.reshape(bsz, seq, 2, 2, heads).transpose(0, 4, 2, 3, 1).reshape(bsz, heads, 4, n_chunks, CHUNK)
    seq_blk = lambda col0: pl.BlockSpec((1, seq, dh), lambda b, h: (b, 0, col0 + h))
    per_chunk = lambda w: pltpu.VMEM((2, n_chunks, w), F32)
    return pl.pallas_call(
        _mlstm_kernel,
        grid=(bsz, heads),
        in_specs=[seq_blk(0), seq_blk(heads), seq_blk(2 * heads), seq_blk(3 * heads),
                  pl.BlockSpec((1, 1, 4, n_chunks, CHUNK), lambda b, h: (b, h, 0, 0, 0)),
                  pl.BlockSpec((1, dh), lambda b, h: (0, h))],
        out_specs=pl.BlockSpec((1, seq, dh), lambda b, h: (b, 0, h)),
        out_shape=jax.ShapeDtypeStruct((bsz, seq, wm), BF16),
        scratch_shapes=[pltpu.VMEM((seq, dh), F32),
                        per_chunk(CHUNK), per_chunk(2 * CHUNK), per_chunk(2 * CHUNK),
                        per_chunk(LANES), per_chunk(LANES), per_chunk(LANES), per_chunk(LANES),
                        pltpu.VMEM((CHUNK, 2 * CHUNK), BF16), pltpu.VMEM((2, CHUNK, CHUNK), F32),
                        pltpu.VMEM((2, dh, dh), F32), pltpu.VMEM((2, 1, dh), F32)],
        compiler_params=_params(("parallel", "parallel"), 58),
        name="mlstm",
    )(qk3, qk3, z3, z3, grow, m_norm_g.reshape(1, wm))


def _merge_kernel(hm_ref, uc_ref, ln_ref, gm_ref, gc_ref, x_ref, mod_ref, wm_ref, wc_ref, wo_ref, *rest, route):
    if route:
        rt_ref, xn_ref, hp_ref, lg_ref = rest
    else:
        xn_ref, hp_ref = rest
    yc = uc_ref[...]
    yc = yc - jnp.mean(yc, axis=-1, keepdims=True)
    var = jnp.mean(yc * yc, axis=-1, keepdims=True)
    u = _silu(yc * lax.rsqrt(var + EPS) * ln_ref[0:1, :] + ln_ref[1:2, :])
    y_m = jnp.dot(hm_ref[...], wm_ref[...], preferred_element_type=F32)
    y_c = jnp.dot(u.astype(BF16), wc_ref[...], preferred_element_type=F32)
    merged = _sigmoid(gm_ref[...].astype(F32)) * y_m + _sigmoid(gc_ref[...].astype(F32)) * y_c
    out = jnp.dot(merged.astype(BF16), wo_ref[...], preferred_element_type=F32)
    x_new = x_ref[...] + mod_ref[0, 2:3, :] * out
    xn_ref[...] = x_new
    h = _rms(x_new) * (1.0 + mod_ref[0, 4:5, :]) + mod_ref[0, 3:4, :]
    hp_ref[...] = _pack_halves(h)
    if route:
        h_hi = h.astype(BF16)
        h_lo = (h - h_hi.astype(F32)).astype(BF16)
        lg_ref[...] = (jnp.dot(h_hi, rt_ref[0], preferred_element_type=F32)
                       + jnp.dot(h_lo, rt_ref[0], preferred_element_type=F32)
                       + jnp.dot(h_hi, rt_ref[1], preferred_element_type=F32))


def _merge(hm, uc, ln_gb, z2, col_gm, x2, mod_l, w_m, w_c, w_o, seq, router_t):
    t, d = x2.shape
    wm_, wc_ = hm.shape[1], uc.shape[1]
    bm = _tile(seq, 256)
    per_b = seq // bm
    assert col_gm % d == 0
    blk = col_gm // d
    route = router_t is not None
    once = pl.Buffered(1)
    in_specs = [pl.BlockSpec((bm, wm_), lambda i: (i, 0)),
                pl.BlockSpec((bm, wc_), lambda i: (i, 0)),
                pl.BlockSpec((2, wc_), lambda i: (0, 0)),
                pl.BlockSpec((bm, d), lambda i: (i, blk)),
                pl.BlockSpec((bm, d), lambda i: (i, blk + 1)),
                pl.BlockSpec((bm, d), lambda i: (i, 0)),
                pl.BlockSpec((1, 6, d), lambda i: (i // per_b, 0, 0)),
                pl.BlockSpec((wm_, d), lambda i: (0, 0), pipeline_mode=once),
                pl.BlockSpec((wc_, d), lambda i: (0, 0), pipeline_mode=once),
                pl.BlockSpec((d, d), lambda i: (0, 0), pipeline_mode=once)]
    out_specs = [pl.BlockSpec((bm, d), lambda i: (i, 0)), pl.BlockSpec((bm, d // 2), lambda i: (i, 0))]
    out_shape = [jax.ShapeDtypeStruct((t, d), F32), jax.ShapeDtypeStruct((t, d // 2), U32)]
    args = [hm, uc, ln_gb, z2, z2, x2, mod_l, w_m, w_c, w_o]
    if route:
        in_specs.append(pl.BlockSpec((2, d, LANES), lambda i: (0, 0, 0)))
        out_specs.append(pl.BlockSpec((bm, LANES), lambda i: (i, 0)))
        out_shape.append(jax.ShapeDtypeStruct((t, LANES), F32))
        args.append(router_t)
    return pl.pallas_call(
        functools.partial(_merge_kernel, route=route),
        grid=(t // bm,),
        in_specs=in_specs, out_specs=out_specs, out_shape=out_shape,
        compiler_params=_params(("parallel",), 48),
        name="merge",
    )(*args)


FFN_ROWS = 512
DENSE_ROWS = 1024
FIRST_OF_GROUP = 1
VALID = 2
HALF = 4


CAST_ROWS = 512


def _stage_group_weights(te_ref, tn_ref, w_hbm, stage_ref, wb_ref, sem):
    n = pl.program_id(0)
    r = pl.program_id(1)
    k_rows, bn = stage_ref.shape[1], stage_ref.shape[2]
    half = w_hbm.shape[2] // 2

    def fetch(e, col_block):
        c0 = col_block * bn
        return [pltpu.make_async_copy(w_hbm.at[e, :, pl.ds(pl.multiple_of(c0 + k * half, LANES), bn)],
                                      stage_ref.at[k], sem.at[k]) for k in range(2)]

    @pl.when((n == 0) & (r == 0))
    def _():
        for cp in fetch(te_ref[0], 0):
            cp.start()

    for cp in fetch(te_ref[r], n):
        cp.wait()

    step_rows = math.gcd(k_rows, CAST_ROWS)

    def cast(i, carry):
        rows = pl.ds(pl.multiple_of(i * step_rows, step_rows), step_rows)
        wb_ref[:, rows, :] = stage_ref[:, rows, :].astype(BF16)
        return carry

    lax.fori_loop(0, k_rows // step_rows, cast, 0)
    nxt = tn_ref[r]

    @pl.when(nxt >= 0)
    def _():
        for cp in fetch(nxt, n):
            cp.start()

    @pl.when((nxt < 0) & (n + 1 < pl.num_programs(0)))
    def _():
        for cp in fetch(te_ref[0], n + 1):
            cp.start()


def _group_scratch(k_rows, bn):
    assert math.gcd(k_rows, CAST_ROWS) % BF16_TILE_ROWS == 0
    return [pltpu.VMEM((2, k_rows, bn), F32), pltpu.VMEM((2, k_rows, bn), BF16), pltpu.SemaphoreType.DMA((2,))]


def _per_tile(flag, o_ref, compute):
    bm = o_ref.shape[0]
    work = flag & (VALID | HALF)

    @pl.when(work == VALID)
    def _():
        compute(slice(0, bm))

    @pl.when(work == (VALID | HALF))
    def _():
        compute(slice(0, bm // 2))
        o_ref[bm // 2:, :] = jnp.zeros((bm - bm // 2, o_ref.shape[1]), o_ref.dtype)

    @pl.when((flag & VALID) == 0)
    def _():
        o_ref[...] = jnp.zeros(o_ref.shape, o_ref.dtype)


def _ffn_up_kernel(te_ref, tf_ref, tn_ref, x_ref, w_hbm, o_ref, stage_ref, wb_ref, sem):
    flag = tf_ref[pl.program_id(1)]

    @pl.when((flag & FIRST_OF_GROUP) != 0)
    def _():
        _stage_group_weights(te_ref, tn_ref, w_hbm, stage_ref, wb_ref, sem)

    def compute(rows):
        lo, hi = _unpack_halves(x_ref[rows, :])
        x = jnp.concatenate([lo.astype(BF16), hi.astype(BF16)], axis=1)
        a = jnp.dot(x, wb_ref[0], preferred_element_type=F32)
        g = jnp.dot(x, wb_ref[1], preferred_element_type=F32)
        o_ref[rows, :] = (_silu(a) * g).astype(o_ref.dtype)

    _per_tile(flag, o_ref, compute)


def _ffn_up(xp, w13, tile_e, tile_f, tile_next, bm):
    rows, dh = xp.shape
    n_e, d, f2 = w13.shape
    f = f2 // 2
    bn = _tile(f, 1024)
    return pl.pallas_call(
        _ffn_up_kernel,
        grid_spec=pltpu.PrefetchScalarGridSpec(
            num_scalar_prefetch=3,
            grid=(f // bn, rows // bm),
            in_specs=[pl.BlockSpec((bm, dh), lambda j, r, te, tf, tn: (r, 0)),
                      pl.BlockSpec(memory_space=pl.ANY)],
            out_specs=pl.BlockSpec((bm, bn), lambda j, r, te, tf, tn: (r, j)),
            scratch_shapes=_group_scratch(d, bn)),
        out_shape=jax.ShapeDtypeStruct((rows, f), BF16),
        compiler_params=_params(("arbitrary", "arbitrary"), 48),
        name="ffn_up",
    )(tile_e, tile_f, tile_next, xp, w13)


def _ffn_down_kernel(te_ref, tf_ref, tn_ref, h_ref, w_hbm, o_ref, stage_ref, wb_ref, sem):
    flag = tf_ref[pl.program_id(1)]

    @pl.when((flag & FIRST_OF_GROUP) != 0)
    def _():
        _stage_group_weights(te_ref, tn_ref, w_hbm, stage_ref, wb_ref, sem)

    def compute(rows):
        h = h_ref[rows, :]
        ya = jnp.dot(h, wb_ref[0], preferred_element_type=F32)
        yb = jnp.dot(h, wb_ref[1], preferred_element_type=F32)
        o_ref[rows, :] = _pack_halves(jnp.concatenate([ya, yb], axis=1))

    _per_tile(flag, o_ref, compute)


def _ffn_down(h, w2, tile_e, tile_f, tile_next, bm):
    rows, f = h.shape
    n_e, _, d = w2.shape
    dh = d // 2
    bn = _tile(dh, 256)
    nn = dh // bn
    return pl.pallas_call(
        _ffn_down_kernel,
        grid_spec=pltpu.PrefetchScalarGridSpec(
            num_scalar_prefetch=3,
            grid=(nn, rows // bm),
            in_specs=[pl.BlockSpec((bm, f), lambda n, r, te, tf, tn: (r, 0)),
                      pl.BlockSpec(memory_space=pl.ANY)],
            out_specs=pl.BlockSpec((bm, bn), lambda n, r, te, tf, tn: (r, n)),
            scratch_shapes=_group_scratch(f, bn)),
        out_shape=jax.ShapeDtypeStruct((rows, dh), U32),
        compiler_params=_params(("arbitrary", "arbitrary"), 48),
        name="ffn_down",
    )(tile_e, tile_f, tile_next, h, w2)


def _residual_epilogue(x, gate, f, nxt_refs, out_refs, final):
    nmod_ref, wg_ref, bg_ref, fin_ref = nxt_refs
    x_new = x + gate * f
    if final:
        out_refs[0][...] = _rms(x_new) * fin_ref[...]
    else:
        out_refs[0][...] = x_new
        _emit_mixer_input(x_new, nmod_ref, wg_ref, bg_ref, out_refs[1], out_refs[2])


def _residual_out(t, d, row, final):
    if final:
        return [row(d)], [jax.ShapeDtypeStruct((t, d), F32)]
    return ([row(d), row(d), row(LANES)],
            [jax.ShapeDtypeStruct((t, d), F32), jax.ShapeDtypeStruct((t, d), BF16),
             jax.ShapeDtypeStruct((t, LANES), F32)])


def _dense_residual_kernel(y_ref, x_ref, mod_ref, nmod_ref, wg_ref, bg_ref, fin_ref, *out_refs, final):
    lo, hi = _unpack_halves(y_ref[...])
    f = jnp.concatenate([lo, hi], axis=1)
    _residual_epilogue(x_ref[...], mod_ref[0, 5:6, :], f, (nmod_ref, wg_ref, bg_ref, fin_ref), out_refs, final)


def _dense_residual(yp, x2, mod_l, nxt, seq, final):
    t, d = x2.shape
    bm = _tile(seq, 512)
    per_b = seq // bm
    mod_spec = pl.BlockSpec((1, 6, d), lambda i: (i // per_b, 0, 0))
    row = lambda w: pl.BlockSpec((bm, w), lambda i: (i, 0))
    out_specs, out_shape = _residual_out(t, d, row, final)
    return pl.pallas_call(
        functools.partial(_dense_residual_kernel, final=final),
        grid=(t // bm,),
        in_specs=[row(d // 2), row(d), mod_spec, mod_spec] + _gate_specs(d, lambda i: (0, 0))
        + [pl.BlockSpec((1, d), lambda i: (0, 0))],
        out_specs=out_specs, out_shape=out_shape,
        compiler_params=_params(("parallel",), 40),
        name="dense_residual",
    )(yp, x2, mod_l, *nxt)


ROUTE_TOKENS = 512


def _route_kernel(lg_ref, idx_ref, wt_ref, cnt_ref, run_ref):
    step = pl.program_id(0)
    n_e, tb = lg_ref.shape

    @pl.when(step == 0)
    def _():
        run_ref[...] = jnp.zeros(run_ref.shape, F32)

    lg = lg_ref[...]
    eid = lax.broadcasted_iota(I32, (n_e, tb), 0)
    m1 = jnp.max(lg, axis=0, keepdims=True)
    e1 = jnp.min(jnp.where(lg == m1, eid, n_e), axis=0, keepdims=True)
    rest = jnp.where(eid == e1, NEG_INF, lg)
    m2 = jnp.max(rest, axis=0, keepdims=True)
    e2 = jnp.min(jnp.where(rest == m2, eid, n_e), axis=0, keepdims=True)
    p2 = jnp.exp(m2 - m1)
    w1 = 1.0 / (1.0 + p2)
    w2 = p2 / (1.0 + p2)
    member = jnp.where((eid == e1) | (eid == e2), 1.0, 0.0)
    before = (lax.broadcasted_iota(I32, (tb, tb), 0) < lax.broadcasted_iota(I32, (tb, tb), 1)).astype(BF16)
    rank = jnp.dot(member.astype(BF16), before, preferred_element_type=F32) + run_ref[:, 0:1]
    r1 = jnp.sum(jnp.where(eid == e1, rank, 0.0), axis=0, keepdims=True)
    r2 = jnp.sum(jnp.where(eid == e2, rank, 0.0), axis=0, keepdims=True)
    zero_i = jnp.zeros((SUBLANES - 4, tb), I32)
    idx_ref[...] = jnp.concatenate([e1, e2, r1.astype(I32), r2.astype(I32), zero_i], axis=0)
    wt_ref[...] = jnp.concatenate([w1, w2, jnp.zeros((SUBLANES - 2, tb), F32)], axis=0)
    run_ref[...] = run_ref[...] + jnp.sum(member, axis=1, keepdims=True)
    cnt_ref[...] = run_ref[...]


def _route(logits_t):
    n_e, t = logits_t.shape
    assert n_e % SUBLANES == 0
    tb = _tile(t, ROUTE_TOKENS)
    return pl.pallas_call(
        _route_kernel,
        grid=(t // tb,),
        in_specs=[pl.BlockSpec((n_e, tb), lambda i: (0, i))],
        out_specs=[pl.BlockSpec((SUBLANES, tb), lambda i: (0, i)),
                   pl.BlockSpec((SUBLANES, tb), lambda i: (0, i)),
                   pl.BlockSpec((n_e, LANES), lambda i: (0, 0))],
        out_shape=[jax.ShapeDtypeStruct((SUBLANES, t), I32), jax.ShapeDtypeStruct((SUBLANES, t), F32),
                   jax.ShapeDtypeStruct((n_e, LANES), F32)],
        scratch_shapes=[pltpu.VMEM((n_e, LANES), F32)],
        compiler_params=_params(("arbitrary",), 32),
        name="route",
    )(logits_t)


MOVE_TOKENS = 256
ISSUE_UNROLL = 8


def _row_copy(src, dst, sem):
    return pltpu.make_async_copy(src, dst, sem)


def _dispatch_kernel(dest_ref, x_ref, init_ref, out_ref, sem):
    del init_ref
    n_tok = x_ref.shape[0]
    total = dest_ref.shape[0] // TOP_K
    base = pl.program_id(0) * n_tok

    def issue(i, carry):
        for k in range(TOP_K):
            _row_copy(x_ref.at[pl.ds(i, 1)], out_ref.at[pl.ds(dest_ref[k * total + base + i], 1)],
                      sem).start(priority=k % 2)
        return carry

    lax.fori_loop(0, n_tok, issue, 0, unroll=ISSUE_UNROLL)
    for k in range(TOP_K):
        _row_copy(x_ref, out_ref.at[pl.ds(0, n_tok)], sem).wait()


def _dispatch(xp, dest_flat, rows):
    t, dh = xp.shape
    nt = _tile(t, MOVE_TOKENS)
    return pl.pallas_call(
        _dispatch_kernel,
        grid_spec=pltpu.PrefetchScalarGridSpec(
            num_scalar_prefetch=1,
            grid=(t // nt,),
            in_specs=[pl.BlockSpec((nt, dh), lambda i, dest: (i, 0)),
                      pl.BlockSpec(memory_space=pl.ANY)],
            out_specs=pl.BlockSpec(memory_space=pl.ANY),
            scratch_shapes=[pltpu.SemaphoreType.DMA(())]),
        out_shape=jax.ShapeDtypeStruct((rows, dh), U32),
        input_output_aliases={2: 0},
        compiler_params=_params(("arbitrary",), 32),
        name="dispatch",
    )(dest_flat, xp, jnp.zeros((rows, dh), U32))


def _combine_kernel(dest_ref, y_ref, wt_ref, x_ref, mod_ref, nmod_ref, wg_ref, bg_ref, fin_ref, *rest, final):
    out_refs, (buf_ref, sem) = rest[:-2], rest[-2:]
    n_tok = x_ref.shape[0]
    total = dest_ref.shape[0] // TOP_K
    step = pl.program_id(0)

    def gather(s):
        slot = s % 2

        def issue(i, carry):
            for k in range(TOP_K):
                _row_copy(y_ref.at[pl.ds(dest_ref[k * total + s * n_tok + i], 1)],
                          buf_ref.at[slot, k, pl.ds(i, 1)], sem.at[slot]).start(priority=k % 2)
            return carry

        lax.fori_loop(0, n_tok, issue, 0, unroll=ISSUE_UNROLL)

    @pl.when(step == 0)
    def _():
        gather(step)

    @pl.when(step + 1 < pl.num_programs(0))
    def _():
        gather(step + 1)

    slot = step % 2
    for k in range(TOP_K):
        _row_copy(y_ref.at[pl.ds(0, n_tok)], buf_ref.at[slot, k], sem.at[slot]).wait()
    f = None
    for k in range(TOP_K):
        lo, hi = _unpack_halves(buf_ref[slot, k])
        part = wt_ref[:, k:k + 1] * jnp.concatenate([lo, hi], axis=1)
        f = part if f is None else f + part
    _residual_epilogue(x_ref[...], mod_ref[0, 5:6, :], f, (nmod_ref, wg_ref, bg_ref, fin_ref), out_refs, final)


def _combine(yp, dest_flat, wt_cols, x2, mod_l, nxt, seq, final):
    t, d = x2.shape
    dh = d // 2
    nt = _tile(seq, MOVE_TOKENS)
    per_b = seq // nt
    mod_spec = pl.BlockSpec((1, 6, d), lambda i, dest: (i // per_b, 0, 0))
    row = lambda w: pl.BlockSpec((nt, w), lambda i, dest: (i, 0))
    out_specs, out_shape = _residual_out(t, d, row, final)
    return pl.pallas_call(
        functools.partial(_combine_kernel, final=final),
        grid_spec=pltpu.PrefetchScalarGridSpec(
            num_scalar_prefetch=1,
            grid=(t // nt,),
            in_specs=[pl.BlockSpec(memory_space=pl.ANY), row(TOP_K), row(d), mod_spec, mod_spec]
            + _gate_specs(d, lambda i, dest: (0, 0)) + [pl.BlockSpec((1, d), lambda i, dest: (0, 0))],
            out_specs=out_specs,
            scratch_shapes=[pltpu.VMEM((2, TOP_K, nt, dh), U32), pltpu.SemaphoreType.DMA((2,))]),
        out_shape=out_shape,
        compiler_params=_params(("arbitrary",), 40),
        name="combine",
    )(dest_flat, yp, wt_cols, x2, mod_l, *nxt)


def _dense_ffn(hp, x2, w13, w2, mod_l, nxt, seq, final):
    t = hp.shape[0]
    bm = math.gcd(t, DENSE_ROWS)
    n_tiles = t // bm
    tile_e = jnp.zeros((n_tiles,), I32)
    tile_f = jnp.full((n_tiles,), VALID, I32).at[0].set(VALID | FIRST_OF_GROUP)
    tile_next = jnp.full((n_tiles,), -1, I32)
    hid = _ffn_up(hp, w13[None], tile_e, tile_f, tile_next, bm)
    yp = _ffn_down(hid, w2[None], tile_e, tile_f, tile_next, bm)
    return _dense_residual(yp, x2, mod_l, nxt, seq, final)


def _moe_ffn(hp, logits_t, x2, w13, w2, mod_l, nxt, seq, final):
    t = hp.shape[0]
    n_e = w13.shape[0]
    idx, wts, cnt = _route(logits_t)
    counts = cnt[:, 0].astype(I32)
    padded = ((counts + FFN_ROWS - 1) // FFN_ROWS) * FFN_ROWS
    seg_end = jnp.cumsum(padded)
    seg_start = seg_end - padded
    n_tiles = (t * TOP_K) // FFN_ROWS + n_e
    tile_row0 = jnp.arange(n_tiles, dtype=I32) * FFN_ROWS
    tile_e = jnp.minimum(jnp.searchsorted(seg_end, tile_row0, side="right"), n_e - 1).astype(I32)
    valid = tile_row0 < seg_end[-1]
    first = jnp.concatenate([jnp.ones((1,), bool), tile_e[1:] != tile_e[:-1]]) & valid
    tokens_in_tile = (seg_start + counts)[tile_e] - tile_row0
    half = valid & (tokens_in_tile <= FFN_ROWS // 2)
    tile_f = (jnp.where(valid, VALID, 0) | jnp.where(first, FIRST_OF_GROUP, 0) | jnp.where(half, HALF, 0)).astype(I32)
    tile_id = jnp.arange(n_tiles, dtype=I32)
    later_start = lax.cummin(jnp.where(first, tile_id, n_tiles), reverse=True)
    next_start = jnp.concatenate([later_start[1:], jnp.full((1,), n_tiles, I32)])
    tile_next = jnp.where(next_start < n_tiles, tile_e[jnp.minimum(next_start, n_tiles - 1)], -1).astype(I32)
    start_of = jnp.sum(jnp.where(idx[:TOP_K, :, None] == jnp.arange(n_e, dtype=I32), seg_start, 0), axis=-1)
    dest = (start_of + idx[TOP_K:2 * TOP_K]).reshape(-1).astype(I32)
    xs = _dispatch(hp, dest, n_tiles * FFN_ROWS)
    hid = _ffn_up(xs, w13, tile_e, tile_f, tile_next, FFN_ROWS)
    yp = _ffn_down(hid, w2, tile_e, tile_f, tile_next, FFN_ROWS)
    return _combine(yp, dest, wts[:TOP_K].T, x2, mod_l, nxt, seq, final)


def kernel(x, c, w_mod, b_mod, w_in, b_in, w_qk_conv, m_norm_g, w_m_proj, w_dw, b_dw, ln_c_g, ln_c_b,
           w_c_proj, w_out, ffn_w13, ffn_w2, moe_router, moe_w13, moe_w2, final_g):
    bsz, seq, d = x.shape
    depth = w_mod.shape[0]
    wm = w_m_proj.shape[1]
    wc = w_dw.shape[2]
    heads = wm // HEAD_DIM
    n_gate = 4 * heads
    off_g = 4 * wm
    off_glu = off_g + n_gate
    t = bsz * seq
    assert d % (2 * LANES) == 0 and n_gate <= LANES

    mods = _adaln_mod(c, w_mod, b_mod)
    x2 = x.reshape(t, d)

    def gate_params(l):
        w_gate = jnp.zeros((d, LANES), BF16).at[:, :n_gate].set(w_in[l][:, off_g:off_glu].astype(BF16))
        b_gate = jnp.zeros((1, LANES), F32).at[0, :n_gate].set(b_in[l][off_g:off_glu])
        return w_gate, b_gate

    h, gates = _prenorm(x2, mods[0], *gate_params(0), seq)
    out = None
    for l in range(depth):
        final = l == depth - 1
        mod_l = mods[l]
        l_next = l if final else l + 1
        nxt = (mods[l_next], *gate_params(l_next), final_g.reshape(1, d))
        w_main = jnp.concatenate([w_in[l][:, :off_g], w_in[l][:, off_glu:]], axis=1).astype(BF16)
        b_main = jnp.concatenate([b_in[l][:off_g], b_in[l][off_glu:]])[None]
        z = _matmul_bias(h, w_main, b_main, BF16)
        gates = gates[:, :n_gate]
        z3 = z.reshape(bsz, seq, z.shape[1])
        k_scale = jnp.concatenate([jnp.ones((wm,), F32), jnp.full((wm,), HEAD_DIM ** -0.5, F32)])
        qk3 = _qk_conv(z3, w_qk_conv[l] * k_scale[None], 2 * wm)
        hm = _mlstm_branch(qk3, z3, gates.reshape(bsz, seq, n_gate), m_norm_g[l], heads)
        glu4 = z[:, off_g:off_g + 2 * wc].reshape(bsz, seq, 2 * wc // LANES, LANES)
        uc = _glu_conv(glu4, w_dw[l], b_dw[l])
        ln_gb = jnp.stack([ln_c_g[l], ln_c_b[l]])
        moe = l % 2 == 1
        router_t = None
        if moe:
            n_e = moe_router.shape[2]
            r_pad = jnp.zeros((d, LANES), F32).at[:, :n_e].set(moe_router[l // 2])
            r_hi = r_pad.astype(BF16)
            router_t = jnp.stack([r_hi, (r_pad - r_hi.astype(F32)).astype(BF16)])
        res = _merge(hm.reshape(t, wm), uc.reshape(t, wc), ln_gb, z, off_g + 2 * wc, x2, mod_l,
                     w_m_proj[l].astype(BF16), w_c_proj[l].astype(BF16), w_out[l].astype(BF16), seq, router_t)
        if moe:
            x2, hp, logits = res
            res = _moe_ffn(hp, logits[:, :n_e].T, x2, moe_w13[l // 2], moe_w2[l // 2], mod_l, nxt, seq, final)
        else:
            x2, hp = res
            res = _dense_ffn(hp, x2, ffn_w13[l // 2], ffn_w2[l // 2], mod_l, nxt, seq, final)
        if final:
            out = res[0]
        else:
            x2, h, gates = res
    return out.reshape(bsz, seq, d)
```

```python
import functools
import math

import jax
import jax.numpy as jnp
from jax import lax
from jax.experimental import pallas as pl
from jax.experimental.pallas import tpu as pltpu

F32 = jnp.float32
BF16 = jnp.bfloat16
U32 = jnp.uint32
I32 = jnp.int32

HEAD_DIM = 256
CHUNK = 128
TOP_K = 2
EPS = 1e-6

V7X_VMEM_BYTES = 64 * 1024 * 1024
LANES = 128
SUBLANES = 8
BF16_TILE_ROWS = 16

HIGHEST = lax.Precision.HIGHEST
NEG_INF = float("-inf")


def _params(semantics, vmem_mib):
    assert vmem_mib * 1024 * 1024 < V7X_VMEM_BYTES
    return pltpu.CompilerParams(dimension_semantics=semantics, vmem_limit_bytes=vmem_mib * 1024 * 1024)


def _tile(n, pref):
    if n <= pref:
        return n
    t = pref - pref % LANES
    while t > LANES and n % t:
        t -= LANES
    assert n % t == 0, (n, pref)
    return t


def _sigmoid(x):
    return 1.0 / (1.0 + jnp.exp(-x))


def _silu(x):
    return x * _sigmoid(x)


def _log_sigmoid(x):
    return jnp.minimum(x, 0.0) - jnp.log1p(jnp.exp(-jnp.abs(x)))


def _rms(x):
    return x * lax.rsqrt(jnp.mean(x * x, axis=-1, keepdims=True) + EPS)


def _pack_halves(y):
    n = y.shape[-1] // 2
    lo = lax.bitcast_convert_type(y[:, :n].astype(BF16).astype(F32), U32)
    hi = lax.bitcast_convert_type(y[:, n:].astype(BF16).astype(F32), U32)
    return (hi & jnp.uint32(0xFFFF0000)) | (lo >> 16)


def _unpack_halves(u):
    lo = lax.bitcast_convert_type(u << 16, F32)
    hi = lax.bitcast_convert_type(u & jnp.uint32(0xFFFF0000), F32)
    return lo, hi


def _mod_kernel(c_ref, w_ref, b_ref, o_ref):
    nb = c_ref.shape[0]
    bn = w_ref.shape[2]
    acts = [_silu(c_ref[b]) for b in range(nb)]
    rows = []
    for b in range(nb):
        tiles = [jnp.sum(w_ref[0, :, j:j + LANES] * acts[b], axis=0, keepdims=True) for j in range(0, bn, LANES)]
        rows.append(jnp.concatenate(tiles, axis=1))
    rows.append(jnp.zeros((SUBLANES - nb, bn), F32))
    o_ref[0] = jnp.concatenate(rows, axis=0) + b_ref[0]


def _adaln_mod(c, w_mod, b_mod):
    depth, d, n = w_mod.shape
    b = c.shape[0]
    assert b < SUBLANES
    c_cols = jnp.broadcast_to(c[:, :, None], (b, d, LANES))
    bn = _tile(n, 1024)
    out = pl.pallas_call(
        _mod_kernel,
        grid=(depth, n // bn),
        in_specs=[pl.BlockSpec((b, d, LANES), lambda l, j: (0, 0, 0)),
                  pl.BlockSpec((1, d, bn), lambda l, j: (l, 0, j)),
                  pl.BlockSpec((1, 1, bn), lambda l, j: (l, 0, j))],
        out_specs=pl.BlockSpec((1, SUBLANES, bn), lambda l, j: (l, 0, j)),
        out_shape=jax.ShapeDtypeStruct((depth, SUBLANES, n), F32),
        compiler_params=_params(("arbitrary", "arbitrary"), 40),
        name="adaln_mod",
    )(c_cols, w_mod, b_mod.reshape(depth, 1, n))
    return out[:, :b].reshape(depth, b, 6, d)


def _emit_mixer_input(x, mod_ref, wg_ref, bg_ref, h_ref, gt_ref):
    h = (_rms(x) * (1.0 + mod_ref[0, 1:2, :]) + mod_ref[0, 0:1, :]).astype(BF16)
    h_ref[...] = h
    gt_ref[...] = jnp.dot(h, wg_ref[...], preferred_element_type=F32) + bg_ref[...]


def _norm_kernel(x_ref, mod_ref, wg_ref, bg_ref, h_ref, gt_ref):
    _emit_mixer_input(x_ref[...], mod_ref, wg_ref, bg_ref, h_ref, gt_ref)


def _gate_specs(d, index):
    return [pl.BlockSpec((d, LANES), index), pl.BlockSpec((1, LANES), index)]


def _prenorm(x2, mod_l, w_gate, b_gate, seq):
    t, d = x2.shape
    bm = _tile(seq, 512)
    per_b = seq // bm
    row = lambda w: pl.BlockSpec((bm, w), lambda i: (i, 0))
    return pl.pallas_call(
        _norm_kernel,
        grid=(t // bm,),
        in_specs=[row(d), pl.BlockSpec((1, 6, d), lambda i: (i // per_b, 0, 0))] + _gate_specs(d, lambda i: (0, 0)),
        out_specs=[row(d), row(LANES)],
        out_shape=[jax.ShapeDtypeStruct((t, d), BF16), jax.ShapeDtypeStruct((t, LANES), F32)],
        compiler_params=_params(("parallel",), 32),
        name="prenorm",
    )(x2, mod_l, w_gate, b_gate)


def _mm_kernel(a_ref, w_ref, b_ref, o_ref):
    acc = jnp.dot(a_ref[...], w_ref[...], preferred_element_type=F32)
    o_ref[...] = (acc + b_ref[...]).astype(o_ref.dtype)


def _matmul_bias(a, w, bias, out_dtype, bm_pref=1024, bn_pref=1024):
    m, k = a.shape
    n = w.shape[1]
    bm = _tile(m, bm_pref)
    bn = _tile(n, bn_pref)
    return pl.pallas_call(
        _mm_kernel,
        grid=(n // bn, m // bm),
        in_specs=[pl.BlockSpec((bm, k), lambda j, i: (i, 0)),
                  pl.BlockSpec((k, bn), lambda j, i: (0, j)),
                  pl.BlockSpec((1, bn), lambda j, i: (0, j))],
        out_specs=pl.BlockSpec((bm, bn), lambda j, i: (i, j)),
        out_shape=jax.ShapeDtypeStruct((m, n), out_dtype),
        compiler_params=_params(("parallel", "arbitrary"), 48),
        name="matmul_bias",
    )(a, w, bias)


CONV_ROWS = 32
CONV_COLS = 256


def _fill_halo(xs_ref, prev, cur, nxt, first, last):
    hb = prev.shape[0]
    bm = cur.shape[0]
    xs_ref[0:hb, :] = jnp.where(first, 0.0, prev)
    xs_ref[hb:hb + bm, :] = cur
    xs_ref[hb + bm:hb + bm + hb, :] = jnp.where(last, 0.0, nxt)


def _conv_rows(xs_ref, w_ref, emit, *, taps, bm, width):
    pad = taps // 2
    lead = BF16_TILE_ROWS - pad
    span = CONV_ROWS + 2 * BF16_TILE_ROWS
    cb = min(CONV_COLS, width)
    assert bm % CONV_ROWS == 0 and width % cb == 0 and lead >= 0

    def body(rb, carry):
        r0 = pl.multiple_of(rb * CONV_ROWS, CONV_ROWS)
        for c0 in range(0, width, cb):
            blk = xs_ref[pl.ds(r0, span), c0:c0 + cb]
            acc = jnp.zeros((CONV_ROWS, cb), F32)
            for k in range(taps):
                acc = acc + w_ref[k:k + 1, c0:c0 + cb] * blk[lead + k:lead + k + CONV_ROWS, :]
            emit(r0, c0, acc)
        return carry

    lax.fori_loop(0, bm // CONV_ROWS, body, 0)


def _qkconv_kernel(prev_ref, cur_ref, next_ref, w_ref, o_ref, xs_ref, *, taps):
    i = pl.program_id(2)
    bm, width = cur_ref.shape[1], cur_ref.shape[2]
    _fill_halo(xs_ref, prev_ref[0].astype(F32), cur_ref[0].astype(F32), next_ref[0].astype(F32),
               i == 0, i == pl.num_programs(2) - 1)

    def emit(r0, c0, acc):
        o_ref[0, pl.ds(r0, CONV_ROWS), c0:c0 + acc.shape[1]] = acc.astype(o_ref.dtype)

    _conv_rows(xs_ref, w_ref, emit, taps=taps, bm=bm, width=width)


def _halo_specs(bm, cb, seq, col0):
    hb = BF16_TILE_ROWS
    per = bm // hb
    n_hb = seq // hb
    return [pl.BlockSpec((1, hb, cb), lambda b, j, i: (b, jnp.maximum(i * per - 1, 0), col0 + j)),
            pl.BlockSpec((1, bm, cb), lambda b, j, i: (b, i, col0 + j)),
            pl.BlockSpec((1, hb, cb), lambda b, j, i: (b, jnp.minimum((i + 1) * per, n_hb - 1), col0 + j))]


def _qk_conv(z3, w_qk, width):
    bsz, seq, _ = z3.shape
    taps = w_qk.shape[0]
    bm = _tile(seq, 512)
    cb = _tile(width, 2048)
    return pl.pallas_call(
        functools.partial(_qkconv_kernel, taps=taps),
        grid=(bsz, width // cb, seq // bm),
        in_specs=_halo_specs(bm, cb, seq, 0) + [pl.BlockSpec((taps, cb), lambda b, j, i: (0, j))],
        out_specs=pl.BlockSpec((1, bm, cb), lambda b, j, i: (b, i, j)),
        out_shape=jax.ShapeDtypeStruct((bsz, seq, width), BF16),
        scratch_shapes=[pltpu.VMEM((bm + 2 * BF16_TILE_ROWS, cb), F32)],
        compiler_params=_params(("parallel", "parallel", "arbitrary"), 32),
        name="qk_conv",
    )(z3, z3, z3, w_qk)


CONV_TOKENS = 8


def _glu_conv_kernel(prev_ref, cur_ref, next_ref, w_ref, b_ref, o_ref, u_ref, *, taps):
    i = pl.program_id(1)
    bm = cur_ref.shape[1]
    g = cur_ref.shape[2] // 2
    hb = BF16_TILE_ROWS
    lead = hb - taps // 2
    assert bm % CONV_TOKENS == 0 and lead >= 0

    def glu(blk):
        x = blk.astype(F32)
        return x[:, :g, :] * _sigmoid(x[:, g:, :])

    u_ref[0:hb] = jnp.where(i == 0, 0.0, glu(prev_ref[0]))
    u_ref[hb:hb + bm] = glu(cur_ref[0])
    u_ref[hb + bm:hb + bm + hb] = jnp.where(i == pl.num_programs(1) - 1, 0.0, glu(next_ref[0]))

    def body(grp, carry):
        t0 = grp * CONV_TOKENS
        acc = [b_ref[...]] * CONV_TOKENS
        for j in range(CONV_TOKENS + taps - 1):
            xin = u_ref[t0 + lead + j]
            for o in range(CONV_TOKENS):
                if 0 <= j - o < taps:
                    acc[o] = acc[o] + w_ref[j - o] * xin
        for o in range(CONV_TOKENS):
            o_ref[0, t0 + o] = acc[o]
        return carry

    lax.fori_loop(0, bm // CONV_TOKENS, body, 0)


def _glu_conv(glu4, w_dw, b_dw):
    bsz, seq, g2, _ = glu4.shape
    g = g2 // 2
    taps = w_dw.shape[0]
    bm = _tile(seq, 512)
    hb = BF16_TILE_ROWS
    per = bm // hb
    n_hb = seq // hb
    return pl.pallas_call(
        functools.partial(_glu_conv_kernel, taps=taps),
        grid=(bsz, seq // bm),
        in_specs=[pl.BlockSpec((1, hb, g2, LANES), lambda b, i: (b, jnp.maximum(i * per - 1, 0), 0, 0)),
                  pl.BlockSpec((1, bm, g2, LANES), lambda b, i: (b, i, 0, 0)),
                  pl.BlockSpec((1, hb, g2, LANES), lambda b, i: (b, jnp.minimum((i + 1) * per, n_hb - 1), 0, 0)),
                  pl.BlockSpec((taps, g, LANES), lambda b, i: (0, 0, 0)),
                  pl.BlockSpec((g, LANES), lambda b, i: (0, 0))],
        out_specs=pl.BlockSpec((1, bm, g, LANES), lambda b, i: (b, i, 0, 0)),
        out_shape=jax.ShapeDtypeStruct((bsz, seq, g, LANES), F32),
        scratch_shapes=[pltpu.VMEM((bm + 2 * hb, g, LANES), F32)],
        compiler_params=_params(("parallel", "arbitrary"), 32),
        name="glu_conv",
    )(glu4, glu4, glu4, w_dw.reshape(taps, g, LANES), b_dw.reshape(g, LANES))


CHUNKS_PER_STEP = 4
FWD, BWD = 0, 1


def _split2(x):
    hi = x.astype(BF16).astype(F32)
    lo = (x - hi).astype(BF16).astype(F32)
    return jnp.concatenate([hi, lo], axis=1)


def _column_forms(eye2, piece_rows):
    rhs = jnp.concatenate([jnp.broadcast_to(p, (LANES, p.shape[1])) for p in piece_rows], axis=0)
    return lax.dot_general(eye2, rhs.astype(BF16), (((1,), (1,)), ((), ())), preferred_element_type=F32)


def _mlstm_direction(q, k, v, a_row, a_col, i_col, b_tot, m_prev, m_new, c_ref, n_ref, bias):
    reps = q.shape[1] // LANES
    wide = lambda x: jnp.concatenate([x] * reps, axis=1)
    b_col = a_col + i_col
    dmat = (b_col - a_row) + bias
    inter = b_col + m_prev
    m_t = jnp.maximum(inter, jnp.max(dmat, axis=-1, keepdims=True))
    w_intra = jnp.exp(dmat - m_t)
    w_inter = jnp.exp(inter - m_t)
    s = lax.dot_general(q, k, (((1,), (1,)), ((), ())), preferred_element_type=F32) * w_intra
    c_prev = c_ref[...]
    n_prev = n_ref[...]
    num = jnp.dot(s.astype(BF16), v, preferred_element_type=F32) + \
        wide(w_inter) * jnp.dot(q, c_prev.astype(BF16), preferred_element_type=F32)
    den = jnp.sum(s, axis=-1, keepdims=True) + \
        w_inter * jnp.sum(q.astype(F32) * n_prev, axis=-1, keepdims=True)
    h = num * wide(1.0 / jnp.maximum(jnp.abs(den), jnp.exp(-m_t)))
    w_s = jnp.exp(b_tot - m_new - a_col)
    decay = wide(jnp.exp(b_tot + m_prev - m_new))
    kw = k.astype(F32) * wide(w_s)
    c_ref[...] = decay * c_prev + lax.dot_general(kw.astype(BF16), v, (((0,), (0,)), ((), ())),
                                                  preferred_element_type=F32)
    n_ref[...] = decay * n_prev + jnp.sum(kw, axis=0, keepdims=True)
    return h


def _mlstm_kernel(q_ref, k_ref, v_ref, o_ref, grow_ref, g_ref, out_ref,
                  hs_ref, a_ref, a2_ref, i2_ref, btot_ref, peak_ref, mprev_ref, mnew_ref,
                  eye_ref, bias_ref, c_ref, n_ref):
    length = CHUNK
    seq = q_ref.shape[1]
    n_chunks = seq // length
    row = lax.broadcasted_iota(I32, (length, length), 0)
    col = lax.broadcasted_iota(I32, (length, length), 1)
    lower = row >= col
    upper = row <= col
    lower_f = jnp.where(lower, 1.0, 0.0)
    upper_f = jnp.where(upper, 1.0, 0.0)
    eye_f = jnp.where(row == col, 1.0, 0.0)
    eye_ref[...] = jnp.concatenate([eye_f, eye_f], axis=1).astype(BF16)
    bias_ref[FWD] = jnp.where(lower, 0.0, NEG_INF)
    bias_ref[BWD] = jnp.where(upper, 0.0, NEG_INF)
    c_ref[...] = jnp.zeros(c_ref.shape, F32)
    n_ref[...] = jnp.zeros(n_ref.shape, F32)

    for d, cum_row, last in ((FWD, upper_f, length - 1), (BWD, lower_f, 0)):
        gate_i = grow_ref[0, 0, 2 * d]
        lf = _log_sigmoid(grow_ref[0, 0, 2 * d + 1])
        b_row = jnp.dot(lf, cum_row, preferred_element_type=F32, precision=HIGHEST)
        a = b_row - gate_i
        a_ref[d] = a
        a2_ref[d] = _split2(a)
        i2_ref[d] = _split2(gate_i)
        btot_ref[d] = jnp.broadcast_to(b_row[:, last:last + 1], (n_chunks, LANES))
        peak_ref[d] = jnp.broadcast_to(jnp.max(-a, axis=-1, keepdims=True), (n_chunks, LANES))

    def stabiliser_scan(c, carry):
        new = []
        for d, m in zip((FWD, BWD), carry):
            i = c if d == FWD else n_chunks - 1 - c
            mprev_ref[d, pl.ds(i, 1), :] = m
            m = btot_ref[d, pl.ds(i, 1), :] + jnp.maximum(m, peak_ref[d, pl.ds(i, 1), :])
            mnew_ref[d, pl.ds(i, 1), :] = m
            new.append(m)
        return tuple(new)

    zero = jnp.zeros((1, LANES), F32)
    lax.fori_loop(0, n_chunks, stabiliser_scan, (zero, zero))

    def run(d, i, a_col, i_col):
        r0 = pl.multiple_of(i * length, length)
        one = pl.ds(i, 1)
        h = _mlstm_direction(q_ref[0, pl.ds(r0, length), :], k_ref[0, pl.ds(r0, length), :],
                             v_ref[0, pl.ds(r0, length), :], a_ref[d, one, :], a_col, i_col,
                             btot_ref[d, one, :], mprev_ref[d, one, :], mnew_ref[d, one, :],
                             c_ref.at[d], n_ref.at[d], bias_ref[d])
        return r0, h

    def chunk_results(step):
        work = []
        for u in range(CHUNKS_PER_STEP):
            c = step * CHUNKS_PER_STEP + u
            work += [(FWD, c), (BWD, n_chunks - 1 - c)]
        rows = []
        for d, i in work:
            rows += [a2_ref[d, pl.ds(i, 1), :], i2_ref[d, pl.ds(i, 1), :]]
        cols = _column_forms(eye_ref[...], rows)
        lanes = lambda j: cols[:, j * LANES:(j + 1) * LANES]
        return [run(d, i, lanes(2 * j), lanes(2 * j + 1)) for j, (d, i) in enumerate(work)]

    def first_touch(step, carry):
        for r0, h in chunk_results(step):
            hs_ref[pl.ds(r0, length), :] = h
        return carry

    def second_touch(step, carry):
        for r0, h in chunk_results(step):
            hs = hs_ref[pl.ds(r0, length), :] + h
            gate = _sigmoid(o_ref[0, pl.ds(r0, length), :].astype(F32))
            out_ref[0, pl.ds(r0, length), :] = ((_rms(hs) * g_ref[...]) * gate).astype(out_ref.dtype)
        return carry

    steps = n_chunks // CHUNKS_PER_STEP
    lax.fori_loop(0, steps // 2, first_touch, 0)
    lax.fori_loop(steps // 2, steps, second_touch, 0)


def _mlstm_branch(qk3, z3, gates, m_norm_g, heads):
    bsz, seq, _ = z3.shape
    dh = HEAD_DIM
    wm = heads * dh
    n_chunks = seq // CHUNK
    assert CHUNK == LANES and dh % LANES == 0
    assert seq % CHUNK == 0 and n_chunks % (2 * CHUNKS_PER_STEP) == 0
    grow = gates.reshape(bsz, seq, 2, 2, heads).transpose(0, 4, 2, 3, 1).reshape(bsz, heads, 4, n_chunks, CHUNK)
    seq_blk = lambda col0: pl.BlockSpec((1, seq, dh), lambda b, h: (b, 0, col0 + h))
    per_chunk = lambda w: pltpu.VMEM((2, n_chunks, w), F32)
    return pl.pallas_call(
        _mlstm_kernel,
        grid=(bsz, heads),
        in_specs=[seq_blk(0), seq_blk(heads), seq_blk(2 * heads), seq_blk(3 * heads),
                  pl.BlockSpec((1, 1, 4, n_chunks, CHUNK), lambda b, h: (b, h, 0, 0, 0)),
                  pl.BlockSpec((1, dh), lambda b, h: (0, h))],
        out_specs=pl.BlockSpec((1, seq, dh), lambda b, h: (b, 0, h)),
        out_shape=jax.ShapeDtypeStruct((bsz, seq, wm), BF16),
        scratch_shapes=[pltpu.VMEM((seq, dh), F32),
                        per_chunk(CHUNK), per_chunk(2 * CHUNK), per_chunk(2 * CHUNK),
                        per_chunk(LANES), per_chunk(LANES), per_chunk(LANES), per_chunk(LANES),
                        pltpu.VMEM((CHUNK, 2 * CHUNK), BF16), pltpu.VMEM((2, CHUNK, CHUNK), F32),
                        pltpu.VMEM((2, dh, dh), F32), pltpu.VMEM((2, 1, dh), F32)],
        compiler_params=_params(("parallel", "parallel"), 58),
        name="mlstm",
    )(qk3, qk3, z3, z3, grow, m_norm_g.reshape(1, wm))


def _merge_kernel(hm_ref, uc_ref, ln_ref, gm_ref, gc_ref, x_ref, mod_ref, wm_ref, wc_ref, wo_ref, *rest, route):
    if route:
        rt_ref, xn_ref, hp_ref, lg_ref = rest
    else:
        xn_ref, hp_ref = rest
    yc = uc_ref[...]
    yc = yc - jnp.mean(yc, axis=-1, keepdims=True)
    var = jnp.mean(yc * yc, axis=-1, keepdims=True)
    u = _silu(yc * lax.rsqrt(var + EPS) * ln_ref[0:1, :] + ln_ref[1:2, :])
    y_m = jnp.dot(hm_ref[...], wm_ref[...], preferred_element_type=F32)
    y_c = jnp.dot(u.astype(BF16), wc_ref[...], preferred_element_type=F32)
    merged = _sigmoid(gm_ref[...].astype(F32)) * y_m + _sigmoid(gc_ref[...].astype(F32)) * y_c
    out = jnp.dot(merged.astype(BF16), wo_ref[...], preferred_element_type=F32)
    x_new = x_ref[...] + mod_ref[0, 2:3, :] * out
    xn_ref[...] = x_new
    h = _rms(x_new) * (1.0 + mod_ref[0, 4:5, :]) + mod_ref[0, 3:4, :]
    hp_ref[...] = _pack_halves(h)
    if route:
        h_hi = h.astype(BF16)
        h_lo = (h - h_hi.astype(F32)).astype(BF16)
        lg_ref[...] = (jnp.dot(h_hi, rt_ref[0], preferred_element_type=F32)
                       + jnp.dot(h_lo, rt_ref[0], preferred_element_type=F32)
                       + jnp.dot(h_hi, rt_ref[1], preferred_element_type=F32))


def _merge(hm, uc, ln_gb, z2, col_gm, x2, mod_l, w_m, w_c, w_o, seq, router_t):
    t, d = x2.shape
    wm_, wc_ = hm.shape[1], uc.shape[1]
    bm = _tile(seq, 256)
    per_b = seq // bm
    assert col_gm % d == 0
    blk = col_gm // d
    route = router_t is not None
    once = pl.Buffered(1)
    in_specs = [pl.BlockSpec((bm, wm_), lambda i: (i, 0)),
                pl.BlockSpec((bm, wc_), lambda i: (i, 0)),
                pl.BlockSpec((2, wc_), lambda i: (0, 0)),
                pl.BlockSpec((bm, d), lambda i: (i, blk)),
                pl.BlockSpec((bm, d), lambda i: (i, blk + 1)),
                pl.BlockSpec((bm, d), lambda i: (i, 0)),
                pl.BlockSpec((1, 6, d), lambda i: (i // per_b, 0, 0)),
                pl.BlockSpec((wm_, d), lambda i: (0, 0), pipeline_mode=once),
                pl.BlockSpec((wc_, d), lambda i: (0, 0), pipeline_mode=once),
                pl.BlockSpec((d, d), lambda i: (0, 0), pipeline_mode=once)]
    out_specs = [pl.BlockSpec((bm, d), lambda i: (i, 0)), pl.BlockSpec((bm, d // 2), lambda i: (i, 0))]
    out_shape = [jax.ShapeDtypeStruct((t, d), F32), jax.ShapeDtypeStruct((t, d // 2), U32)]
    args = [hm, uc, ln_gb, z2, z2, x2, mod_l, w_m, w_c, w_o]
    if route:
        in_specs.append(pl.BlockSpec((2, d, LANES), lambda i: (0, 0, 0)))
        out_specs.append(pl.BlockSpec((bm, LANES), lambda i: (i, 0)))
        out_shape.append(jax.ShapeDtypeStruct((t, LANES), F32))
        args.append(router_t)
    return pl.pallas_call(
        functools.partial(_merge_kernel, route=route),
        grid=(t // bm,),
        in_specs=in_specs, out_specs=out_specs, out_shape=out_shape,
        compiler_params=_params(("parallel",), 48),
        name="merge",
    )(*args)


FFN_ROWS = 512
DENSE_ROWS = 1024
FIRST_OF_GROUP = 1
VALID = 2
HALF = 4


CAST_ROWS = 512


def _stage_group_weights(te_ref, tn_ref, w_hbm, stage_ref, wb_ref, sem):
    n = pl.program_id(0)
    r = pl.program_id(1)
    k_rows, bn = stage_ref.shape[1], stage_ref.shape[2]
    half = w_hbm.shape[2] // 2

    def fetch(e, col_block):
        c0 = col_block * bn
        return [pltpu.make_async_copy(w_hbm.at[e, :, pl.ds(pl.multiple_of(c0 + k * half, LANES), bn)],
                                      stage_ref.at[k], sem.at[k]) for k in range(2)]

    @pl.when((n == 0) & (r == 0))
    def _():
        for cp in fetch(te_ref[0], 0):
            cp.start()

    for cp in fetch(te_ref[r], n):
        cp.wait()

    step_rows = math.gcd(k_rows, CAST_ROWS)

    def cast(i, carry):
        rows = pl.ds(pl.multiple_of(i * step_rows, step_rows), step_rows)
        wb_ref[:, rows, :] = stage_ref[:, rows, :].astype(BF16)
        return carry

    lax.fori_loop(0, k_rows // step_rows, cast, 0)
    nxt = tn_ref[r]

    @pl.when(nxt >= 0)
    def _():
        for cp in fetch(nxt, n):
            cp.start()

    @pl.when((nxt < 0) & (n + 1 < pl.num_programs(0)))
    def _():
        for cp in fetch(te_ref[0], n + 1):
            cp.start()


def _group_scratch(k_rows, bn):
    assert math.gcd(k_rows, CAST_ROWS) % BF16_TILE_ROWS == 0
    return [pltpu.VMEM((2, k_rows, bn), F32), pltpu.VMEM((2, k_rows, bn), BF16), pltpu.SemaphoreType.DMA((2,))]


def _per_tile(flag, o_ref, compute):
    bm = o_ref.shape[0]
    work = flag & (VALID | HALF)

    @pl.when(work == VALID)
    def _():
        compute(slice(0, bm))

    @pl.when(work == (VALID | HALF))
    def _():
        compute(slice(0, bm // 2))
        o_ref[bm // 2:, :] = jnp.zeros((bm - bm // 2, o_ref.shape[1]), o_ref.dtype)

    @pl.when((flag & VALID) == 0)
    def _():
        o_ref[...] = jnp.zeros(o_ref.shape, o_ref.dtype)


def _ffn_up_kernel(te_ref, tf_ref, tn_ref, x_ref, w_hbm, o_ref, stage_ref, wb_ref, sem):
    flag = tf_ref[pl.program_id(1)]

    @pl.when((flag & FIRST_OF_GROUP) != 0)
    def _():
        _stage_group_weights(te_ref, tn_ref, w_hbm, stage_ref, wb_ref, sem)

    def compute(rows):
        lo, hi = _unpack_halves(x_ref[rows, :])
        x = jnp.concatenate([lo.astype(BF16), hi.astype(BF16)], axis=1)
        a = jnp.dot(x, wb_ref[0], preferred_element_type=F32)
        g = jnp.dot(x, wb_ref[1], preferred_element_type=F32)
        o_ref[rows, :] = (_silu(a) * g).astype(o_ref.dtype)

    _per_tile(flag, o_ref, compute)


def _ffn_up(xp, w13, tile_e, tile_f, tile_next, bm):
    rows, dh = xp.shape
    n_e, d, f2 = w13.shape
    f = f2 // 2
    bn = _tile(f, 1024)
    return pl.pallas_call(
        _ffn_up_kernel,
        grid_spec=pltpu.PrefetchScalarGridSpec(
            num_scalar_prefetch=3,
            grid=(f // bn, rows // bm),
            in_specs=[pl.BlockSpec((bm, dh), lambda j, r, te, tf, tn: (r, 0)),
                      pl.BlockSpec(memory_space=pl.ANY)],
            out_specs=pl.BlockSpec((bm, bn), lambda j, r, te, tf, tn: (r, j)),
            scratch_shapes=_group_scratch(d, bn)),
        out_shape=jax.ShapeDtypeStruct((rows, f), BF16),
        compiler_params=_params(("arbitrary", "arbitrary"), 48),
        name="ffn_up",
    )(tile_e, tile_f, tile_next, xp, w13)


def _ffn_down_kernel(te_ref, tf_ref, tn_ref, h_ref, w_hbm, o_ref, stage_ref, wb_ref, sem):
    flag = tf_ref[pl.program_id(1)]

    @pl.when((flag & FIRST_OF_GROUP) != 0)
    def _():
        _stage_group_weights(te_ref, tn_ref, w_hbm, stage_ref, wb_ref, sem)

    def compute(rows):
        h = h_ref[rows, :]
        ya = jnp.dot(h, wb_ref[0], preferred_element_type=F32)
        yb = jnp.dot(h, wb_ref[1], preferred_element_type=F32)
        o_ref[rows, :] = _pack_halves(jnp.concatenate([ya, yb], axis=1))

    _per_tile(flag, o_ref, compute)


def _ffn_down(h, w2, tile_e, tile_f, tile_next, bm):
    rows, f = h.shape
    n_e, _, d = w2.shape
    dh = d // 2
    bn = _tile(dh, 256)
    nn = dh // bn
    return pl.pallas_call(
        _ffn_down_kernel,
        grid_spec=pltpu.PrefetchScalarGridSpec(
            num_scalar_prefetch=3,
            grid=(nn, rows // bm),
            in_specs=[pl.BlockSpec((bm, f), lambda n, r, te, tf, tn: (r, 0)),
                      pl.BlockSpec(memory_space=pl.ANY)],
            out_specs=pl.BlockSpec((bm, bn), lambda n, r, te, tf, tn: (r, n)),
            scratch_shapes=_group_scratch(f, bn)),
        out_shape=jax.ShapeDtypeStruct((rows, dh), U32),
        compiler_params=_params(("arbitrary", "arbitrary"), 48),
        name="ffn_down",
    )(tile_e, tile_f, tile_next, h, w2)


def _residual_epilogue(x, gate, f, nxt_refs, out_refs, final):
    nmod_ref, wg_ref, bg_ref, fin_ref = nxt_refs
    x_new = x + gate * f
    if final:
        out_refs[0][...] = _rms(x_new) * fin_ref[...]
    else:
        out_refs[0][...] = x_new
        _emit_mixer_input(x_new, nmod_ref, wg_ref, bg_ref, out_refs[1], out_refs[2])


def _residual_out(t, d, row, final):
    if final:
        return [row(d)], [jax.ShapeDtypeStruct((t, d), F32)]
    return ([row(d), row(d), row(LANES)],
            [jax.ShapeDtypeStruct((t, d), F32), jax.ShapeDtypeStruct((t, d), BF16),
             jax.ShapeDtypeStruct((t, LANES), F32)])


def _dense_residual_kernel(y_ref, x_ref, mod_ref, nmod_ref, wg_ref, bg_ref, fin_ref, *out_refs, final):
    lo, hi = _unpack_halves(y_ref[...])
    f = jnp.concatenate([lo, hi], axis=1)
    _residual_epilogue(x_ref[...], mod_ref[0, 5:6, :], f, (nmod_ref, wg_ref, bg_ref, fin_ref), out_refs, final)


def _dense_residual(yp, x2, mod_l, nxt, seq, final):
    t, d = x2.shape
    bm = _tile(seq, 512)
    per_b = seq // bm
    mod_spec = pl.BlockSpec((1, 6, d), lambda i: (i // per_b, 0, 0))
    row = lambda w: pl.BlockSpec((bm, w), lambda i: (i, 0))
    out_specs, out_shape = _residual_out(t, d, row, final)
    return pl.pallas_call(
        functools.partial(_dense_residual_kernel, final=final),
        grid=(t // bm,),
        in_specs=[row(d // 2), row(d), mod_spec, mod_spec] + _gate_specs(d, lambda i: (0, 0))
        + [pl.BlockSpec((1, d), lambda i: (0, 0))],
        out_specs=out_specs, out_shape=out_shape,
        compiler_params=_params(("parallel",), 40),
        name="dense_residual",
    )(yp, x2, mod_l, *nxt)


ROUTE_TOKENS = 512


def _route_kernel(lg_ref, idx_ref, wt_ref, cnt_ref, run_ref):
    step = pl.program_id(0)
    n_e, tb = lg_ref.shape

    @pl.when(step == 0)
    def _():
        run_ref[...] = jnp.zeros(run_ref.shape, F32)

    lg = lg_ref[...]
    eid = lax.broadcasted_iota(I32, (n_e, tb), 0)
    m1 = jnp.max(lg, axis=0, keepdims=True)
    e1 = jnp.min(jnp.where(lg == m1, eid, n_e), axis=0, keepdims=True)
    rest = jnp.where(eid == e1, NEG_INF, lg)
    m2 = jnp.max(rest, axis=0, keepdims=True)
    e2 = jnp.min(jnp.where(rest == m2, eid, n_e), axis=0, keepdims=True)
    p2 = jnp.exp(m2 - m1)
    w1 = 1.0 / (1.0 + p2)
    w2 = p2 / (1.0 + p2)
    member = jnp.where((eid == e1) | (eid == e2), 1.0, 0.0)
    before = (lax.broadcasted_iota(I32, (tb, tb), 0) < lax.broadcasted_iota(I32, (tb, tb), 1)).astype(BF16)
    rank = jnp.dot(member.astype(BF16), before, preferred_element_type=F32) + run_ref[:, 0:1]
    r1 = jnp.sum(jnp.where(eid == e1, rank, 0.0), axis=0, keepdims=True)
    r2 = jnp.sum(jnp.where(eid == e2, rank, 0.0), axis=0, keepdims=True)
    zero_i = jnp.zeros((SUBLANES - 4, tb), I32)
    idx_ref[...] = jnp.concatenate([e1, e2, r1.astype(I32), r2.astype(I32), zero_i], axis=0)
    wt_ref[...] = jnp.concatenate([w1, w2, jnp.zeros((SUBLANES - 2, tb), F32)], axis=0)
    run_ref[...] = run_ref[...] + jnp.sum(member, axis=1, keepdims=True)
    cnt_ref[...] = run_ref[...]


def _route(logits_t):
    n_e, t = logits_t.shape
    assert n_e % SUBLANES == 0
    tb = _tile(t, ROUTE_TOKENS)
    return pl.pallas_call(
        _route_kernel,
        grid=(t // tb,),
        in_specs=[pl.BlockSpec((n_e, tb), lambda i: (0, i))],
        out_specs=[pl.BlockSpec((SUBLANES, tb), lambda i: (0, i)),
                   pl.BlockSpec((SUBLANES, tb), lambda i: (0, i)),
                   pl.BlockSpec((n_e, LANES), lambda i: (0, 0))],
        out_shape=[jax.ShapeDtypeStruct((SUBLANES, t), I32), jax.ShapeDtypeStruct((SUBLANES, t), F32),
                   jax.ShapeDtypeStruct((n_e, LANES), F32)],
        scratch_shapes=[pltpu.VMEM((n_e, LANES), F32)],
        compiler_params=_params(("arbitrary",), 32),
        name="route",
    )(logits_t)


MOVE_TOKENS = 256
ISSUE_UNROLL = 8


def _row_copy(src, dst, sem):
    return pltpu.make_async_copy(src, dst, sem)


def _dispatch_kernel(dest_ref, x_ref, init_ref, out_ref, sem):
    del init_ref
    n_tok = x_ref.shape[0]
    total = dest_ref.shape[0] // TOP_K
    base = pl.program_id(0) * n_tok

    def issue(i, carry):
        for k in range(TOP_K):
            _row_copy(x_ref.at[pl.ds(i, 1)], out_ref.at[pl.ds(dest_ref[k * total + base + i], 1)],
                      sem).start(priority=k % 2)
        return carry

    lax.fori_loop(0, n_tok, issue, 0, unroll=ISSUE_UNROLL)
    for k in range(TOP_K):
        _row_copy(x_ref, out_ref.at[pl.ds(0, n_tok)], sem).wait()


def _dispatch(xp, dest_flat, rows):
    t, dh = xp.shape
    nt = _tile(t, MOVE_TOKENS)
    return pl.pallas_call(
        _dispatch_kernel,
        grid_spec=pltpu.PrefetchScalarGridSpec(
            num_scalar_prefetch=1,
            grid=(t // nt,),
            in_specs=[pl.BlockSpec((nt, dh), lambda i, dest: (i, 0)),
                      pl.BlockSpec(memory_space=pl.ANY)],
            out_specs=pl.BlockSpec(memory_space=pl.ANY),
            scratch_shapes=[pltpu.SemaphoreType.DMA(())]),
        out_shape=jax.ShapeDtypeStruct((rows, dh), U32),
        input_output_aliases={2: 0},
        compiler_params=_params(("arbitrary",), 32),
        name="dispatch",
    )(dest_flat, xp, jnp.zeros((rows, dh), U32))


def _combine_kernel(dest_ref, y_ref, wt_ref, x_ref, mod_ref, nmod_ref, wg_ref, bg_ref, fin_ref, *rest, final):
    out_refs, (buf_ref, sem) = rest[:-2], rest[-2:]
    n_tok = x_ref.shape[0]
    total = dest_ref.shape[0] // TOP_K
    step = pl.program_id(0)

    def gather(s):
        slot = s % 2

        def issue(i, carry):
            for k in range(TOP_K):
                _row_copy(y_ref.at[pl.ds(dest_ref[k * total + s * n_tok + i], 1)],
                          buf_ref.at[slot, k, pl.ds(i, 1)], sem.at[slot]).start(priority=k % 2)
            return carry

        lax.fori_loop(0, n_tok, issue, 0, unroll=ISSUE_UNROLL)

    @pl.when(step == 0)
    def _():
        gather(step)

    @pl.when(step + 1 < pl.num_programs(0))
    def _():
        gather(step + 1)

    slot = step % 2
    for k in range(TOP_K):
        _row_copy(y_ref.at[pl.ds(0, n_tok)], buf_ref.at[slot, k], sem.at[slot]).wait()
    f = None
    for k in range(TOP_K):
        lo, hi = _unpack_halves(buf_ref[slot, k])
        part = wt_ref[:, k:k + 1] * jnp.concatenate([lo, hi], axis=1)
        f = part if f is None else f + part
    _residual_epilogue(x_ref[...], mod_ref[0, 5:6, :], f, (nmod_ref, wg_ref, bg_ref, fin_ref), out_refs, final)


def _combine(yp, dest_flat, wt_cols, x2, mod_l, nxt, seq, final):
    t, d = x2.shape
    dh = d // 2
    nt = _tile(seq, MOVE_TOKENS)
    per_b = seq // nt
    mod_spec = pl.BlockSpec((1, 6, d), lambda i, dest: (i // per_b, 0, 0))
    row = lambda w: pl.BlockSpec((nt, w), lambda i, dest: (i, 0))
    out_specs, out_shape = _residual_out(t, d, row, final)
    return pl.pallas_call(
        functools.partial(_combine_kernel, final=final),
        grid_spec=pltpu.PrefetchScalarGridSpec(
            num_scalar_prefetch=1,
            grid=(t // nt,),
            in_specs=[pl.BlockSpec(memory_space=pl.ANY), row(TOP_K), row(d), mod_spec, mod_spec]
            + _gate_specs(d, lambda i, dest: (0, 0)) + [pl.BlockSpec((1, d), lambda i, dest: (0, 0))],
            out_specs=out_specs,
            scratch_shapes=[pltpu.VMEM((2, TOP_K, nt, dh), U32), pltpu.SemaphoreType.DMA((2,))]),
        out_shape=out_shape,
        compiler_params=_params(("arbitrary",), 40),
        name="combine",
    )(dest_flat, yp, wt_cols, x2, mod_l, *nxt)


def _dense_ffn(hp, x2, w13, w2, mod_l, nxt, seq, final):
    t = hp.shape[0]
    bm = math.gcd(t, DENSE_ROWS)
    n_tiles = t // bm
    tile_e = jnp.zeros((n_tiles,), I32)
    tile_f = jnp.full((n_tiles,), VALID, I32).at[0].set(VALID | FIRST_OF_GROUP)
    tile_next = jnp.full((n_tiles,), -1, I32)
    hid = _ffn_up(hp, w13[None], tile_e, tile_f, tile_next, bm)
    yp = _ffn_down(hid, w2[None], tile_e, tile_f, tile_next, bm)
    return _dense_residual(yp, x2, mod_l, nxt, seq, final)


def _moe_ffn(hp, logits_t, x2, w13, w2, mod_l, nxt, seq, final):
    t = hp.shape[0]
    n_e = w13.shape[0]
    idx, wts, cnt = _route(logits_t)
    counts = cnt[:, 0].astype(I32)
    padded = ((counts + FFN_ROWS - 1) // FFN_ROWS) * FFN_ROWS
    seg_end = jnp.cumsum(padded)
    seg_start = seg_end - padded
    n_tiles = (t * TOP_K) // FFN_ROWS + n_e
    tile_row0 = jnp.arange(n_tiles, dtype=I32) * FFN_ROWS
    tile_e = jnp.minimum(jnp.searchsorted(seg_end, tile_row0, side="right"), n_e - 1).astype(I32)
    valid = tile_row0 < seg_end[-1]
    first = jnp.concatenate([jnp.ones((1,), bool), tile_e[1:] != tile_e[:-1]]) & valid
    tokens_in_tile = (seg_start + counts)[tile_e] - tile_row0
    half = valid & (tokens_in_tile <= FFN_ROWS // 2)
    tile_f = (jnp.where(valid, VALID, 0) | jnp.where(first, FIRST_OF_GROUP, 0) | jnp.where(half, HALF, 0)).astype(I32)
    tile_id = jnp.arange(n_tiles, dtype=I32)
    later_start = lax.cummin(jnp.where(first, tile_id, n_tiles), reverse=True)
    next_start = jnp.concatenate([later_start[1:], jnp.full((1,), n_tiles, I32)])
    tile_next = jnp.where(next_start < n_tiles, tile_e[jnp.minimum(next_start, n_tiles - 1)], -1).astype(I32)
    start_of = jnp.sum(jnp.where(idx[:TOP_K, :, None] == jnp.arange(n_e, dtype=I32), seg_start, 0), axis=-1)
    dest = (start_of + idx[TOP_K:2 * TOP_K]).reshape(-1).astype(I32)
    xs = _dispatch(hp, dest, n_tiles * FFN_ROWS)
    hid = _ffn_up(xs, w13, tile_e, tile_f, tile_next, FFN_ROWS)
    yp = _ffn_down(hid, w2, tile_e, tile_f, tile_next, FFN_ROWS)
    return _combine(yp, dest, wts[:TOP_K].T, x2, mod_l, nxt, seq, final)


def kernel(x, c, w_mod, b_mod, w_in, b_in, w_qk_conv, m_norm_g, w_m_proj, w_dw, b_dw, ln_c_g, ln_c_b,
           w_c_proj, w_out, ffn_w13, ffn_w2, moe_router, moe_w13, moe_w2, final_g):
    bsz, seq, d = x.shape
    depth = w_mod.shape[0]
    wm = w_m_proj.shape[1]
    wc = w_dw.shape[2]
    heads = wm // HEAD_DIM
    n_gate = 4 * heads
    off_g = 4 * wm
    off_glu = off_g + n_gate
    t = bsz * seq
    assert d % (2 * LANES) == 0 and n_gate <= LANES

    mods = _adaln_mod(c, w_mod, b_mod)
    x2 = x.reshape(t, d)

    def gate_params(l):
        w_gate = jnp.zeros((d, LANES), BF16).at[:, :n_gate].set(w_in[l][:, off_g:off_glu].astype(BF16))
        b_gate = jnp.zeros((1, LANES), F32).at[0, :n_gate].set(b_in[l][off_g:off_glu])
        return w_gate, b_gate

    h, gates = _prenorm(x2, mods[0], *gate_params(0), seq)
    out = None
    for l in range(depth):
        final = l == depth - 1
        mod_l = mods[l]
        l_next = l if final else l + 1
        nxt = (mods[l_next], *gate_params(l_next), final_g.reshape(1, d))
        w_main = jnp.concatenate([w_in[l][:, :off_g], w_in[l][:, off_glu:]], axis=1).astype(BF16)
        b_main = jnp.concatenate([b_in[l][:off_g], b_in[l][off_glu:]])[None]
        z = _matmul_bias(h, w_main, b_main, BF16)
        gates = gates[:, :n_gate]
        z3 = z.reshape(bsz, seq, z.shape[1])
        k_scale = jnp.concatenate([jnp.ones((wm,), F32), jnp.full((wm,), HEAD_DIM ** -0.5, F32)])
        qk3 = _qk_conv(z3, w_qk_conv[l] * k_scale[None], 2 * wm)
        hm = _mlstm_branch(qk3, z3, gates.reshape(bsz, seq, n_gate), m_norm_g[l], heads)
        glu4 = z[:, off_g:off_g + 2 * wc].reshape(bsz, seq, 2 * wc // LANES, LANES)
        uc = _glu_conv(glu4, w_dw[l], b_dw[l])
        ln_gb = jnp.stack([ln_c_g[l], ln_c_b[l]])
        moe = l % 2 == 1
        router_t = None
        if moe:
            n_e = moe_router.shape[2]
            r_pad = jnp.zeros((d, LANES), F32).at[:, :n_e].set(moe_router[l // 2])
            r_hi = r_pad.astype(BF16)
            router_t = jnp.stack([r_hi, (r_pad - r_hi.astype(F32)).astype(BF16)])
        res = _merge(hm.reshape(t, wm), uc.reshape(t, wc), ln_gb, z, off_g + 2 * wc, x2, mod_l,
                     w_m_proj[l].astype(BF16), w_c_proj[l].astype(BF16), w_out[l].astype(BF16), seq, router_t)
        if moe:
            x2, hp, logits = res
            res = _moe_ffn(hp, logits[:, :n_e].T, x2, moe_w13[l // 2], moe_w2[l // 2], mod_l, nxt, seq, final)
        else:
            x2, hp = res
            res = _dense_ffn(hp, x2, ffn_w13[l // 2], ffn_w2[l // 2], mod_l, nxt, seq, final)
        if final:
            out = res[0]
        else:
            x2, h, gates = res
    return out.reshape(bsz, seq, d)
```

```python
import functools
import math

import jax
import jax.numpy as jnp
from jax import lax
from jax.experimental import pallas as pl
from jax.experimental.pallas import tpu as pltpu

F32 = jnp.float32
BF16 = jnp.bfloat16
U32 = jnp.uint32
I32 = jnp.int32

HEAD_DIM = 256
CHUNK = 128
TOP_K = 2
EPS = 1e-6

V7X_VMEM_BYTES = 64 * 1024 * 1024
LANES = 128
SUBLANES = 8
BF16_TILE_ROWS = 16

HIGHEST = lax.Precision.HIGHEST
NEG_INF = float("-inf")


def _params(semantics, vmem_mib):
    assert vmem_mib * 1024 * 1024 < V7X_VMEM_BYTES
    return pltpu.CompilerParams(dimension_semantics=semantics, vmem_limit_bytes=vmem_mib * 1024 * 1024)


def _tile(n, pref):
    if n <= pref:
        return n
    t = pref - pref % LANES
    while t > LANES and n % t:
        t -= LANES
    assert n % t == 0, (n, pref)
    return t


def _sigmoid(x):
    return 1.0 / (1.0 + jnp.exp(-x))


def _silu(x):
    return x * _sigmoid(x)


def _log_sigmoid(x):
    return jnp.minimum(x, 0.0) - jnp.log1p(jnp.exp(-jnp.abs(x)))


def _rms(x):
    return x * lax.rsqrt(jnp.mean(x * x, axis=-1, keepdims=True) + EPS)


def _pack_halves(y):
    n = y.shape[-1] // 2
    lo = lax.bitcast_convert_type(y[:, :n].astype(BF16).astype(F32), U32)
    hi = lax.bitcast_convert_type(y[:, n:].astype(BF16).astype(F32), U32)
    return (hi & jnp.uint32(0xFFFF0000)) | (lo >> 16)


def _unpack_halves(u):
    lo = lax.bitcast_convert_type(u << 16, F32)
    hi = lax.bitcast_convert_type(u & jnp.uint32(0xFFFF0000), F32)
    return lo, hi


def _mod_kernel(c_ref, w_ref, b_ref, o_ref):
    nb = c_ref.shape[0]
    bn = w_ref.shape[2]
    acts = [_silu(c_ref[b]) for b in range(nb)]
    rows = []
    for b in range(nb):
        tiles = [jnp.sum(w_ref[0, :, j:j + LANES] * acts[b], axis=0, keepdims=True) for j in range(0, bn, LANES)]
        rows.append(jnp.concatenate(tiles, axis=1))
    rows.append(jnp.zeros((SUBLANES - nb, bn), F32))
    o_ref[0] = jnp.concatenate(rows, axis=0) + b_ref[0]


def _adaln_mod(c, w_mod, b_mod):
    depth, d, n = w_mod.shape
    b = c.shape[0]
    assert b < SUBLANES
    c_cols = jnp.broadcast_to(c[:, :, None], (b, d, LANES))
    bn = _tile(n, 2048)
    out = pl.pallas_call(
        _mod_kernel,
        grid=(depth, n // bn),
        in_specs=[pl.BlockSpec((b, d, LANES), lambda l, j: (0, 0, 0)),
                  pl.BlockSpec((1, d, bn), lambda l, j: (l, 0, j)),
                  pl.BlockSpec((1, 1, bn), lambda l, j: (l, 0, j))],
        out_specs=pl.BlockSpec((1, SUBLANES, bn), lambda l, j: (l, 0, j)),
        out_shape=jax.ShapeDtypeStruct((depth, SUBLANES, n), F32),
        compiler_params=_params(("arbitrary", "arbitrary"), 40),
        name="adaln_mod",
    )(c_cols, w_mod, b_mod.reshape(depth, 1, n))
    return out[:, :b].reshape(depth, b, 6, d)


def _emit_mixer_input(x, mod_ref, wg_ref, bg_ref, h_ref, gt_ref):
    h = (_rms(x) * (1.0 + mod_ref[0, 1:2, :]) + mod_ref[0, 0:1, :]).astype(BF16)
    h_ref[...] = h
    gt_ref[...] = jnp.dot(h, wg_ref[...], preferred_element_type=F32) + bg_ref[...]


def _norm_kernel(x_ref, mod_ref, wg_ref, bg_ref, h_ref, gt_ref):
    _emit_mixer_input(x_ref[...], mod_ref, wg_ref, bg_ref, h_ref, gt_ref)


def _gate_specs(d, index):
    return [pl.BlockSpec((d, LANES), index), pl.BlockSpec((1, LANES), index)]


def _prenorm(x2, mod_l, w_gate, b_gate, seq):
    t, d = x2.shape
    bm = _tile(seq, 512)
    per_b = seq // bm
    row = lambda w: pl.BlockSpec((bm, w), lambda i: (i, 0))
    return pl.pallas_call(
        _norm_kernel,
        grid=(t // bm,),
        in_specs=[row(d), pl.BlockSpec((1, 6, d), lambda i: (i // per_b, 0, 0))] + _gate_specs(d, lambda i: (0, 0)),
        out_specs=[row(d), row(LANES)],
        out_shape=[jax.ShapeDtypeStruct((t, d), BF16), jax.ShapeDtypeStruct((t, LANES), F32)],
        compiler_params=_params(("parallel",), 32),
        name="prenorm",
    )(x2, mod_l, w_gate, b_gate)


def _mm_kernel(a_ref, w_ref, b_ref, o_ref):
    acc = jnp.dot(a_ref[...], w_ref[...], preferred_element_type=F32)
    o_ref[...] = (acc + b_ref[...]).astype(o_ref.dtype)


def _matmul_bias(a, w, bias, out_dtype, bm_pref=1024, bn_pref=1024):
    m, k = a.shape
    n = w.shape[1]
    bm = _tile(m, bm_pref)
    bn = _tile(n, bn_pref)
    return pl.pallas_call(
        _mm_kernel,
        grid=(n // bn, m // bm),
        in_specs=[pl.BlockSpec((bm, k), lambda j, i: (i, 0)),
                  pl.BlockSpec((k, bn), lambda j, i: (0, j)),
                  pl.BlockSpec((1, bn), lambda j, i: (0, j))],
        out_specs=pl.BlockSpec((bm, bn), lambda j, i: (i, j)),
        out_shape=jax.ShapeDtypeStruct((m, n), out_dtype),
        compiler_params=_params(("parallel", "arbitrary"), 48),
        name="matmul_bias",
    )(a, w, bias)


CONV_ROWS = 32
CONV_COLS = 256


def _fill_halo(xs_ref, prev, cur, nxt, first, last):
    hb = prev.shape[0]
    bm = cur.shape[0]
    xs_ref[0:hb, :] = jnp.where(first, 0.0, prev)
    xs_ref[hb:hb + bm, :] = cur
    xs_ref[hb + bm:hb + bm + hb, :] = jnp.where(last, 0.0, nxt)


def _conv_rows(xs_ref, w_ref, emit, *, taps, bm, width):
    pad = taps // 2
    lead = BF16_TILE_ROWS - pad
    span = CONV_ROWS + 2 * BF16_TILE_ROWS
    cb = min(CONV_COLS, width)
    assert bm % CONV_ROWS == 0 and width % cb == 0 and lead >= 0

    def body(rb, carry):
        r0 = pl.multiple_of(rb * CONV_ROWS, CONV_ROWS)
        for c0 in range(0, width, cb):
            blk = xs_ref[pl.ds(r0, span), c0:c0 + cb]
            acc = jnp.zeros((CONV_ROWS, cb), F32)
            for k in range(taps):
                acc = acc + w_ref[k:k + 1, c0:c0 + cb] * blk[lead + k:lead + k + CONV_ROWS, :]
            emit(r0, c0, acc)
        return carry

    lax.fori_loop(0, bm // CONV_ROWS, body, 0)


def _qkconv_kernel(prev_ref, cur_ref, next_ref, w_ref, o_ref, xs_ref, *, taps):
    i = pl.program_id(2)
    bm, width = cur_ref.shape[1], cur_ref.shape[2]
    _fill_halo(xs_ref, prev_ref[0].astype(F32), cur_ref[0].astype(F32), next_ref[0].astype(F32),
               i == 0, i == pl.num_programs(2) - 1)

    def emit(r0, c0, acc):
        o_ref[0, pl.ds(r0, CONV_ROWS), c0:c0 + acc.shape[1]] = acc.astype(o_ref.dtype)

    _conv_rows(xs_ref, w_ref, emit, taps=taps, bm=bm, width=width)


def _halo_specs(bm, cb, seq, col0):
    hb = BF16_TILE_ROWS
    per = bm // hb
    n_hb = seq // hb
    return [pl.BlockSpec((1, hb, cb), lambda b, j, i: (b, jnp.maximum(i * per - 1, 0), col0 + j)),
            pl.BlockSpec((1, bm, cb), lambda b, j, i: (b, i, col0 + j)),
            pl.BlockSpec((1, hb, cb), lambda b, j, i: (b, jnp.minimum((i + 1) * per, n_hb - 1), col0 + j))]


def _qk_conv(z3, w_qk, width):
    bsz, seq, _ = z3.shape
    taps = w_qk.shape[0]
    bm = _tile(seq, 512)
    cb = _tile(width, 2048)
    return pl.pallas_call(
        functools.partial(_qkconv_kernel, taps=taps),
        grid=(bsz, width // cb, seq // bm),
        in_specs=_halo_specs(bm, cb, seq, 0) + [pl.BlockSpec((taps, cb), lambda b, j, i: (0, j))],
        out_specs=pl.BlockSpec((1, bm, cb), lambda b, j, i: (b, i, j)),
        out_shape=jax.ShapeDtypeStruct((bsz, seq, width), BF16),
        scratch_shapes=[pltpu.VMEM((bm + 2 * BF16_TILE_ROWS, cb), F32)],
        compiler_params=_params(("parallel", "parallel", "arbitrary"), 32),
        name="qk_conv",
    )(z3, z3, z3, w_qk)


CONV_TOKENS = 8


def _glu_conv_kernel(prev_ref, cur_ref, next_ref, w_ref, b_ref, o_ref, u_ref, *, taps):
    i = pl.program_id(1)
    bm = cur_ref.shape[1]
    g = cur_ref.shape[2] // 2
    hb = BF16_TILE_ROWS
    lead = hb - taps // 2
    assert bm % CONV_TOKENS == 0 and lead >= 0

    def glu(blk):
        x = blk.astype(F32)
        return x[:, :g, :] * _sigmoid(x[:, g:, :])

    u_ref[0:hb] = jnp.where(i == 0, 0.0, glu(prev_ref[0]))
    u_ref[hb:hb + bm] = glu(cur_ref[0])
    u_ref[hb + bm:hb + bm + hb] = jnp.where(i == pl.num_programs(1) - 1, 0.0, glu(next_ref[0]))

    def body(grp, carry):
        t0 = grp * CONV_TOKENS
        acc = [b_ref[...]] * CONV_TOKENS
        for j in range(CONV_TOKENS + taps - 1):
            xin = u_ref[t0 + lead + j]
            for o in range(CONV_TOKENS):
                if 0 <= j - o < taps:
                    acc[o] = acc[o] + w_ref[j - o] * xin
        for o in range(CONV_TOKENS):
            o_ref[0, t0 + o] = acc[o]
        return carry

    lax.fori_loop(0, bm // CONV_TOKENS, body, 0)


def _glu_conv(glu4, w_dw, b_dw):
    bsz, seq, g2, _ = glu4.shape
    g = g2 // 2
    taps = w_dw.shape[0]
    bm = _tile(seq, 512)
    hb = BF16_TILE_ROWS
    per = bm // hb
    n_hb = seq // hb
    return pl.pallas_call(
        functools.partial(_glu_conv_kernel, taps=taps),
        grid=(bsz, seq // bm),
        in_specs=[pl.BlockSpec((1, hb, g2, LANES), lambda b, i: (b, jnp.maximum(i * per - 1, 0), 0, 0)),
                  pl.BlockSpec((1, bm, g2, LANES), lambda b, i: (b, i, 0, 0)),
                  pl.BlockSpec((1, hb, g2, LANES), lambda b, i: (b, jnp.minimum((i + 1) * per, n_hb - 1), 0, 0)),
                  pl.BlockSpec((taps, g, LANES), lambda b, i: (0, 0, 0)),
                  pl.BlockSpec((g, LANES), lambda b, i: (0, 0))],
        out_specs=pl.BlockSpec((1, bm, g, LANES), lambda b, i: (b, i, 0, 0)),
        out_shape=jax.ShapeDtypeStruct((bsz, seq, g, LANES), F32),
        scratch_shapes=[pltpu.VMEM((bm + 2 * hb, g, LANES), F32)],
        compiler_params=_params(("parallel", "arbitrary"), 32),
        name="glu_conv",
    )(glu4, glu4, glu4, w_dw.reshape(taps, g, LANES), b_dw.reshape(g, LANES))


CHUNKS_PER_STEP = 4
FWD, BWD = 0, 1


def _split2(x):
    hi = x.astype(BF16).astype(F32)
    lo = (x - hi).astype(BF16).astype(F32)
    return jnp.concatenate([hi, lo], axis=1)


def _column_forms(eye2, piece_rows):
    rhs = jnp.concatenate([jnp.broadcast_to(p, (LANES, p.shape[1])) for p in piece_rows], axis=0)
    return lax.dot_general(eye2, rhs.astype(BF16), (((1,), (1,)), ((), ())), preferred_element_type=F32)


def _mlstm_direction(q, k, v, a_row, a_col, i_col, b_tot, m_prev, m_new, c_ref, n_ref, bias):
    reps = q.shape[1] // LANES
    wide = lambda x: jnp.concatenate([x] * reps, axis=1)
    b_col = a_col + i_col
    dmat = (b_col - a_row) + bias
    inter = b_col + m_prev
    m_t = jnp.maximum(inter, jnp.max(dmat, axis=-1, keepdims=True))
    w_intra = jnp.exp(dmat - m_t)
    w_inter = jnp.exp(inter - m_t)
    s = lax.dot_general(q, k, (((1,), (1,)), ((), ())), preferred_element_type=F32) * w_intra
    c_prev = c_ref[...]
    n_prev = n_ref[...]
    num = jnp.dot(s.astype(BF16), v, preferred_element_type=F32) + \
        wide(w_inter) * jnp.dot(q, c_prev.astype(BF16), preferred_element_type=F32)
    den = jnp.sum(s, axis=-1, keepdims=True) + \
        w_inter * jnp.sum(q.astype(F32) * n_prev, axis=-1, keepdims=True)
    h = num * wide(1.0 / jnp.maximum(jnp.abs(den), jnp.exp(-m_t)))
    w_s = jnp.exp(b_tot - m_new - a_col)
    decay = wide(jnp.exp(b_tot + m_prev - m_new))
    kw = k.astype(F32) * wide(w_s)
    c_ref[...] = decay * c_prev + lax.dot_general(kw.astype(BF16), v, (((0,), (0,)), ((), ())),
                                                  preferred_element_type=F32)
    n_ref[...] = decay * n_prev + jnp.sum(kw, axis=0, keepdims=True)
    return h


def _mlstm_kernel(q_ref, k_ref, v_ref, o_ref, grow_ref, g_ref, out_ref,
                  hs_ref, a_ref, a2_ref, i2_ref, btot_ref, peak_ref, mprev_ref, mnew_ref,
                  eye_ref, bias_ref, c_ref, n_ref):
    length = CHUNK
    seq = q_ref.shape[1]
    n_chunks = seq // length
    row = lax.broadcasted_iota(I32, (length, length), 0)
    col = lax.broadcasted_iota(I32, (length, length), 1)
    lower = row >= col
    upper = row <= col
    lower_f = jnp.where(lower, 1.0, 0.0)
    upper_f = jnp.where(upper, 1.0, 0.0)
    eye_f = jnp.where(row == col, 1.0, 0.0)
    eye_ref[...] = jnp.concatenate([eye_f, eye_f], axis=1).astype(BF16)
    bias_ref[FWD] = jnp.where(lower, 0.0, NEG_INF)
    bias_ref[BWD] = jnp.where(upper, 0.0, NEG_INF)
    c_ref[...] = jnp.zeros(c_ref.shape, F32)
    n_ref[...] = jnp.zeros(n_ref.shape, F32)

    for d, cum_row, last in ((FWD, upper_f, length - 1), (BWD, lower_f, 0)):
        gate_i = grow_ref[0, 0, 2 * d]
        lf = _log_sigmoid(grow_ref[0, 0, 2 * d + 1])
        b_row = jnp.dot(lf, cum_row, preferred_element_type=F32, precision=HIGHEST)
        a = b_row - gate_i
        a_ref[d] = a
        a2_ref[d] = _split2(a)
        i2_ref[d] = _split2(gate_i)
        btot_ref[d] = jnp.broadcast_to(b_row[:, last:last + 1], (n_chunks, LANES))
        peak_ref[d] = jnp.broadcast_to(jnp.max(-a, axis=-1, keepdims=True), (n_chunks, LANES))

    def stabiliser_scan(c, carry):
        new = []
        for d, m in zip((FWD, BWD), carry):
            i = c if d == FWD else n_chunks - 1 - c
            mprev_ref[d, pl.ds(i, 1), :] = m
            m = btot_ref[d, pl.ds(i, 1), :] + jnp.maximum(m, peak_ref[d, pl.ds(i, 1), :])
            mnew_ref[d, pl.ds(i, 1), :] = m
            new.append(m)
        return tuple(new)

    zero = jnp.zeros((1, LANES), F32)
    lax.fori_loop(0, n_chunks, stabiliser_scan, (zero, zero))

    def run(d, i, a_col, i_col):
        r0 = pl.multiple_of(i * length, length)
        one = pl.ds(i, 1)
        h = _mlstm_direction(q_ref[0, pl.ds(r0, length), :], k_ref[0, pl.ds(r0, length), :],
                             v_ref[0, pl.ds(r0, length), :], a_ref[d, one, :], a_col, i_col,
                             btot_ref[d, one, :], mprev_ref[d, one, :], mnew_ref[d, one, :],
                             c_ref.at[d], n_ref.at[d], bias_ref[d])
        return r0, h

    def chunk_results(step):
        work = []
        for u in range(CHUNKS_PER_STEP):
            c = step * CHUNKS_PER_STEP + u
            work += [(FWD, c), (BWD, n_chunks - 1 - c)]
        rows = []
        for d, i in work:
            rows += [a2_ref[d, pl.ds(i, 1), :], i2_ref[d, pl.ds(i, 1), :]]
        cols = _column_forms(eye_ref[...], rows)
        lanes = lambda j: cols[:, j * LANES:(j + 1) * LANES]
        return [run(d, i, lanes(2 * j), lanes(2 * j + 1)) for j, (d, i) in enumerate(work)]

    def first_touch(step, carry):
        for r0, h in chunk_results(step):
            hs_ref[pl.ds(r0, length), :] = h
        return carry

    def second_touch(step, carry):
        for r0, h in chunk_results(step):
            hs = hs_ref[pl.ds(r0, length), :] + h
            gate = _sigmoid(o_ref[0, pl.ds(r0, length), :].astype(F32))
            out_ref[0, pl.ds(r0, length), :] = ((_rms(hs) * g_ref[...]) * gate).astype(out_ref.dtype)
        return carry

    steps = n_chunks // CHUNKS_PER_STEP
    lax.fori_loop(0, steps // 2, first_touch, 0)
    lax.fori_loop(steps // 2, steps, second_touch, 0)


def _mlstm_branch(qk3, z3, gates, m_norm_g, heads):
    bsz, seq, _ = z3.shape
    dh = HEAD_DIM
    wm = heads * dh
    n_chunks = seq // CHUNK
    assert CHUNK == LANES and dh % LANES == 0
    assert seq % CHUNK == 0 and n_chunks % (2 * CHUNKS_PER_STEP) == 0
    grow = gates.reshape(bsz, seq, 2, 2, heads).transpose(0, 4, 2, 3, 1).reshape(bsz, heads, 4, n_chunks, CHUNK)
    seq_blk = lambda col0: pl.BlockSpec((1, seq, dh), lambda b, h: (b, 0, col0 + h))
    per_chunk = lambda w: pltpu.VMEM((2, n_chunks, w), F32)
    return pl.pallas_call(
        _mlstm_kernel,
        grid=(bsz, heads),
        in_specs=[seq_blk(0), seq_blk(heads), seq_blk(2 * heads), seq_blk(3 * heads),
                  pl.BlockSpec((1, 1, 4, n_chunks, CHUNK), lambda b, h: (b, h, 0, 0, 0)),
                  pl.BlockSpec((1, dh), lambda b, h: (0, h))],
        out_specs=pl.BlockSpec((1, seq, dh), lambda b, h: (b, 0, h)),
        out_shape=jax.ShapeDtypeStruct((bsz, seq, wm), BF16),
        scratch_shapes=[pltpu.VMEM((seq, dh), F32),
                        per_chunk(CHUNK), per_chunk(2 * CHUNK), per_chunk(2 * CHUNK),
                        per_chunk(LANES), per_chunk(LANES), per_chunk(LANES), per_chunk(LANES),
                        pltpu.VMEM((CHUNK, 2 * CHUNK), BF16), pltpu.VMEM((2, CHUNK, CHUNK), F32),
                        pltpu.VMEM((2, dh, dh), F32), pltpu.VMEM((2, 1, dh), F32)],
        compiler_params=_params(("parallel", "parallel"), 58),
        name="mlstm",
    )(qk3, qk3, z3, z3, grow, m_norm_g.reshape(1, wm))


def _merge_kernel(hm_ref, uc_ref, ln_ref, gm_ref, gc_ref, x_ref, mod_ref, wm_ref, wc_ref, wo_ref, *rest, route):
    if route:
        rt_ref, xn_ref, hp_ref, lg_ref = rest
    else:
        xn_ref, hp_ref = rest
    yc = uc_ref[...]
    yc = yc - jnp.mean(yc, axis=-1, keepdims=True)
    var = jnp.mean(yc * yc, axis=-1, keepdims=True)
    u = _silu(yc * lax.rsqrt(var + EPS) * ln_ref[0:1, :] + ln_ref[1:2, :])
    y_m = jnp.dot(hm_ref[...], wm_ref[...], preferred_element_type=F32)
    y_c = jnp.dot(u.astype(BF16), wc_ref[...], preferred_element_type=F32)
    merged = _sigmoid(gm_ref[...].astype(F32)) * y_m + _sigmoid(gc_ref[...].astype(F32)) * y_c
    out = jnp.dot(merged.astype(BF16), wo_ref[...], preferred_element_type=F32)
    x_new = x_ref[...] + mod_ref[0, 2:3, :] * out
    xn_ref[...] = x_new
    h = _rms(x_new) * (1.0 + mod_ref[0, 4:5, :]) + mod_ref[0, 3:4, :]
    hp_ref[...] = _pack_halves(h)
    if route:
        h_hi = h.astype(BF16)
        h_lo = (h - h_hi.astype(F32)).astype(BF16)
        lg_ref[...] = (jnp.dot(h_hi, rt_ref[0], preferred_element_type=F32)
                       + jnp.dot(h_lo, rt_ref[0], preferred_element_type=F32)
                       + jnp.dot(h_hi, rt_ref[1], preferred_element_type=F32))


def _merge(hm, uc, ln_gb, z2, col_gm, x2, mod_l, w_m, w_c, w_o, seq, router_t):
    t, d = x2.shape
    wm_, wc_ = hm.shape[1], uc.shape[1]
    bm = _tile(seq, 256)
    per_b = seq // bm
    assert col_gm % d == 0
    blk = col_gm // d
    route = router_t is not None
    once = pl.Buffered(1)
    in_specs = [pl.BlockSpec((bm, wm_), lambda i: (i, 0)),
                pl.BlockSpec((bm, wc_), lambda i: (i, 0)),
                pl.BlockSpec((2, wc_), lambda i: (0, 0)),
                pl.BlockSpec((bm, d), lambda i: (i, blk)),
                pl.BlockSpec((bm, d), lambda i: (i, blk + 1)),
                pl.BlockSpec((bm, d), lambda i: (i, 0)),
                pl.BlockSpec((1, 6, d), lambda i: (i // per_b, 0, 0)),
                pl.BlockSpec((wm_, d), lambda i: (0, 0), pipeline_mode=once),
                pl.BlockSpec((wc_, d), lambda i: (0, 0), pipeline_mode=once),
                pl.BlockSpec((d, d), lambda i: (0, 0), pipeline_mode=once)]
    out_specs = [pl.BlockSpec((bm, d), lambda i: (i, 0)), pl.BlockSpec((bm, d // 2), lambda i: (i, 0))]
    out_shape = [jax.ShapeDtypeStruct((t, d), F32), jax.ShapeDtypeStruct((t, d // 2), U32)]
    args = [hm, uc, ln_gb, z2, z2, x2, mod_l, w_m, w_c, w_o]
    if route:
        in_specs.append(pl.BlockSpec((2, d, LANES), lambda i: (0, 0, 0)))
        out_specs.append(pl.BlockSpec((bm, LANES), lambda i: (i, 0)))
        out_shape.append(jax.ShapeDtypeStruct((t, LANES), F32))
        args.append(router_t)
    return pl.pallas_call(
        functools.partial(_merge_kernel, route=route),
        grid=(t // bm,),
        in_specs=in_specs, out_specs=out_specs, out_shape=out_shape,
        compiler_params=_params(("parallel",), 48),
        name="merge",
    )(*args)


FFN_ROWS = 512
DENSE_ROWS = 1024
FIRST_OF_GROUP = 1
VALID = 2
HALF = 4


CAST_ROWS = 512


def _stage_group_weights(te_ref, tn_ref, w_hbm, stage_ref, wb_ref, sem):
    n = pl.program_id(0)
    r = pl.program_id(1)
    k_rows, bn = stage_ref.shape[1], stage_ref.shape[2]
    half = w_hbm.shape[2] // 2

    def fetch(e, col_block):
        c0 = col_block * bn
        return [pltpu.make_async_copy(w_hbm.at[e, :, pl.ds(pl.multiple_of(c0 + k * half, LANES), bn)],
                                      stage_ref.at[k], sem.at[k]) for k in range(2)]

    @pl.when((n == 0) & (r == 0))
    def _():
        for cp in fetch(te_ref[0], 0):
            cp.start()

    for cp in fetch(te_ref[r], n):
        cp.wait()

    step_rows = math.gcd(k_rows, CAST_ROWS)

    def cast(i, carry):
        rows = pl.ds(pl.multiple_of(i * step_rows, step_rows), step_rows)
        wb_ref[:, rows, :] = stage_ref[:, rows, :].astype(BF16)
        return carry

    lax.fori_loop(0, k_rows // step_rows, cast, 0)
    nxt = tn_ref[r]

    @pl.when(nxt >= 0)
    def _():
        for cp in fetch(nxt, n):
            cp.start()

    @pl.when((nxt < 0) & (n + 1 < pl.num_programs(0)))
    def _():
        for cp in fetch(te_ref[0], n + 1):
            cp.start()


def _group_scratch(k_rows, bn):
    assert math.gcd(k_rows, CAST_ROWS) % BF16_TILE_ROWS == 0
    return [pltpu.VMEM((2, k_rows, bn), F32), pltpu.VMEM((2, k_rows, bn), BF16), pltpu.SemaphoreType.DMA((2,))]


def _per_tile(flag, o_ref, compute):
    bm = o_ref.shape[0]
    work = flag & (VALID | HALF)

    @pl.when(work == VALID)
    def _():
        compute(slice(0, bm))

    @pl.when(work == (VALID | HALF))
    def _():
        compute(slice(0, bm // 2))
        o_ref[bm // 2:, :] = jnp.zeros((bm - bm // 2, o_ref.shape[1]), o_ref.dtype)

    @pl.when((flag & VALID) == 0)
    def _():
        o_ref[...] = jnp.zeros(o_ref.shape, o_ref.dtype)


def _ffn_up_kernel(te_ref, tf_ref, tn_ref, x_ref, w_hbm, o_ref, stage_ref, wb_ref, sem):
    flag = tf_ref[pl.program_id(1)]

    @pl.when((flag & FIRST_OF_GROUP) != 0)
    def _():
        _stage_group_weights(te_ref, tn_ref, w_hbm, stage_ref, wb_ref, sem)

    def compute(rows):
        lo, hi = _unpack_halves(x_ref[rows, :])
        x = jnp.concatenate([lo.astype(BF16), hi.astype(BF16)], axis=1)
        a = jnp.dot(x, wb_ref[0], preferred_element_type=F32)
        g = jnp.dot(x, wb_ref[1], preferred_element_type=F32)
        o_ref[rows, :] = (_silu(a) * g).astype(o_ref.dtype)

    _per_tile(flag, o_ref, compute)


def _ffn_up(xp, w13, tile_e, tile_f, tile_next, bm):
    rows, dh = xp.shape
    n_e, d, f2 = w13.shape
    f = f2 // 2
    bn = _tile(f, 1024)
    return pl.pallas_call(
        _ffn_up_kernel,
        grid_spec=pltpu.PrefetchScalarGridSpec(
            num_scalar_prefetch=3,
            grid=(f // bn, rows // bm),
            in_specs=[pl.BlockSpec((bm, dh), lambda j, r, te, tf, tn: (r, 0)),
                      pl.BlockSpec(memory_space=pl.ANY)],
            out_specs=pl.BlockSpec((bm, bn), lambda j, r, te, tf, tn: (r, j)),
            scratch_shapes=_group_scratch(d, bn)),
        out_shape=jax.ShapeDtypeStruct((rows, f), BF16),
        compiler_params=_params(("arbitrary", "arbitrary"), 48),
        name="ffn_up",
    )(tile_e, tile_f, tile_next, xp, w13)


def _ffn_down_kernel(te_ref, tf_ref, tn_ref, h_ref, w_hbm, o_ref, stage_ref, wb_ref, sem):
    flag = tf_ref[pl.program_id(1)]

    @pl.when((flag & FIRST_OF_GROUP) != 0)
    def _():
        _stage_group_weights(te_ref, tn_ref, w_hbm, stage_ref, wb_ref, sem)

    def compute(rows):
        h = h_ref[rows, :]
        ya = jnp.dot(h, wb_ref[0], preferred_element_type=F32)
        yb = jnp.dot(h, wb_ref[1], preferred_element_type=F32)
        o_ref[rows, :] = _pack_halves(jnp.concatenate([ya, yb], axis=1))

    _per_tile(flag, o_ref, compute)


def _ffn_down(h, w2, tile_e, tile_f, tile_next, bm):
    rows, f = h.shape
    n_e, _, d = w2.shape
    dh = d // 2
    bn = _tile(dh, 256)
    nn = dh // bn
    return pl.pallas_call(
        _ffn_down_kernel,
        grid_spec=pltpu.PrefetchScalarGridSpec(
            num_scalar_prefetch=3,
            grid=(nn, rows // bm),
            in_specs=[pl.BlockSpec((bm, f), lambda n, r, te, tf, tn: (r, 0)),
                      pl.BlockSpec(memory_space=pl.ANY)],
            out_specs=pl.BlockSpec((bm, bn), lambda n, r, te, tf, tn: (r, n)),
            scratch_shapes=_group_scratch(f, bn)),
        out_shape=jax.ShapeDtypeStruct((rows, dh), U32),
        compiler_params=_params(("arbitrary", "arbitrary"), 48),
        name="ffn_down",
    )(tile_e, tile_f, tile_next, h, w2)


def _residual_epilogue(x, gate, f, nxt_refs, out_refs, final):
    nmod_ref, wg_ref, bg_ref, fin_ref = nxt_refs
    x_new = x + gate * f
    if final:
        out_refs[0][...] = _rms(x_new) * fin_ref[...]
    else:
        out_refs[0][...] = x_new
        _emit_mixer_input(x_new, nmod_ref, wg_ref, bg_ref, out_refs[1], out_refs[2])


def _residual_out(t, d, row, final):
    if final:
        return [row(d)], [jax.ShapeDtypeStruct((t, d), F32)]
    return ([row(d), row(d), row(LANES)],
            [jax.ShapeDtypeStruct((t, d), F32), jax.ShapeDtypeStruct((t, d), BF16),
             jax.ShapeDtypeStruct((t, LANES), F32)])


def _dense_residual_kernel(y_ref, x_ref, mod_ref, nmod_ref, wg_ref, bg_ref, fin_ref, *out_refs, final):
    lo, hi = _unpack_halves(y_ref[...])
    f = jnp.concatenate([lo, hi], axis=1)
    _residual_epilogue(x_ref[...], mod_ref[0, 5:6, :], f, (nmod_ref, wg_ref, bg_ref, fin_ref), out_refs, final)


def _dense_residual(yp, x2, mod_l, nxt, seq, final):
    t, d = x2.shape
    bm = _tile(seq, 512)
    per_b = seq // bm
    mod_spec = pl.BlockSpec((1, 6, d), lambda i: (i // per_b, 0, 0))
    row = lambda w: pl.BlockSpec((bm, w), lambda i: (i, 0))
    out_specs, out_shape = _residual_out(t, d, row, final)
    return pl.pallas_call(
        functools.partial(_dense_residual_kernel, final=final),
        grid=(t // bm,),
        in_specs=[row(d // 2), row(d), mod_spec, mod_spec] + _gate_specs(d, lambda i: (0, 0))
        + [pl.BlockSpec((1, d), lambda i: (0, 0))],
        out_specs=out_specs, out_shape=out_shape,
        compiler_params=_params(("parallel",), 40),
        name="dense_residual",
    )(yp, x2, mod_l, *nxt)


ROUTE_TOKENS = 512


def _route_kernel(lg_ref, idx_ref, wt_ref, cnt_ref, run_ref):
    step = pl.program_id(0)
    n_e, tb = lg_ref.shape

    @pl.when(step == 0)
    def _():
        run_ref[...] = jnp.zeros(run_ref.shape, F32)

    lg = lg_ref[...]
    eid = lax.broadcasted_iota(I32, (n_e, tb), 0)
    m1 = jnp.max(lg, axis=0, keepdims=True)
    e1 = jnp.min(jnp.where(lg == m1, eid, n_e), axis=0, keepdims=True)
    rest = jnp.where(eid == e1, NEG_INF, lg)
    m2 = jnp.max(rest, axis=0, keepdims=True)
    e2 = jnp.min(jnp.where(rest == m2, eid, n_e), axis=0, keepdims=True)
    p2 = jnp.exp(m2 - m1)
    w1 = 1.0 / (1.0 + p2)
    w2 = p2 / (1.0 + p2)
    member = jnp.where((eid == e1) | (eid == e2), 1.0, 0.0)
    before = (lax.broadcasted_iota(I32, (tb, tb), 0) < lax.broadcasted_iota(I32, (tb, tb), 1)).astype(BF16)
    rank = jnp.dot(member.astype(BF16), before, preferred_element_type=F32) + run_ref[:, 0:1]
    r1 = jnp.sum(jnp.where(eid == e1, rank, 0.0), axis=0, keepdims=True)
    r2 = jnp.sum(jnp.where(eid == e2, rank, 0.0), axis=0, keepdims=True)
    zero_i = jnp.zeros((SUBLANES - 4, tb), I32)
    idx_ref[...] = jnp.concatenate([e1, e2, r1.astype(I32), r2.astype(I32), zero_i], axis=0)
    wt_ref[...] = jnp.concatenate([w1, w2, jnp.zeros((SUBLANES - 2, tb), F32)], axis=0)
    run_ref[...] = run_ref[...] + jnp.sum(member, axis=1, keepdims=True)
    cnt_ref[...] = run_ref[...]


def _route(logits_t):
    n_e, t = logits_t.shape
    assert n_e % SUBLANES == 0
    tb = _tile(t, ROUTE_TOKENS)
    return pl.pallas_call(
        _route_kernel,
        grid=(t // tb,),
        in_specs=[pl.BlockSpec((n_e, tb), lambda i: (0, i))],
        out_specs=[pl.BlockSpec((SUBLANES, tb), lambda i: (0, i)),
                   pl.BlockSpec((SUBLANES, tb), lambda i: (0, i)),
                   pl.BlockSpec((n_e, LANES), lambda i: (0, 0))],
        out_shape=[jax.ShapeDtypeStruct((SUBLANES, t), I32), jax.ShapeDtypeStruct((SUBLANES, t), F32),
                   jax.ShapeDtypeStruct((n_e, LANES), F32)],
        scratch_shapes=[pltpu.VMEM((n_e, LANES), F32)],
        compiler_params=_params(("arbitrary",), 32),
        name="route",
    )(logits_t)


MOVE_TOKENS = 256
ISSUE_UNROLL = 8


def _row_copy(src, dst, sem):
    return pltpu.make_async_copy(src, dst, sem)


def _dispatch_kernel(dest_ref, x_ref, init_ref, out_ref, sem):
    del init_ref
    n_tok = x_ref.shape[0]
    total = dest_ref.shape[0] // TOP_K
    base = pl.program_id(0) * n_tok

    def issue(i, carry):
        for k in range(TOP_K):
            _row_copy(x_ref.at[pl.ds(i, 1)], out_ref.at[pl.ds(dest_ref[k * total + base + i], 1)],
                      sem).start(priority=k % 2)
        return carry

    lax.fori_loop(0, n_tok, issue, 0, unroll=ISSUE_UNROLL)
    for k in range(TOP_K):
        _row_copy(x_ref, out_ref.at[pl.ds(0, n_tok)], sem).wait()


def _dispatch(xp, dest_flat, rows):
    t, dh = xp.shape
    nt = _tile(t, MOVE_TOKENS)
    return pl.pallas_call(
        _dispatch_kernel,
        grid_spec=pltpu.PrefetchScalarGridSpec(
            num_scalar_prefetch=1,
            grid=(t // nt,),
            in_specs=[pl.BlockSpec((nt, dh), lambda i, dest: (i, 0)),
                      pl.BlockSpec(memory_space=pl.ANY)],
            out_specs=pl.BlockSpec(memory_space=pl.ANY),
            scratch_shapes=[pltpu.SemaphoreType.DMA(())]),
        out_shape=jax.ShapeDtypeStruct((rows, dh), U32),
        input_output_aliases={2: 0},
        compiler_params=_params(("arbitrary",), 32),
        name="dispatch",
    )(dest_flat, xp, jnp.zeros((rows, dh), U32))


def _combine_kernel(dest_ref, y_ref, wt_ref, x_ref, mod_ref, nmod_ref, wg_ref, bg_ref, fin_ref, *rest, final):
    out_refs, (buf_ref, sem) = rest[:-2], rest[-2:]
    n_tok = x_ref.shape[0]
    total = dest_ref.shape[0] // TOP_K
    step = pl.program_id(0)

    def gather(s):
        slot = s % 2

        def issue(i, carry):
            for k in range(TOP_K):
                _row_copy(y_ref.at[pl.ds(dest_ref[k * total + s * n_tok + i], 1)],
                          buf_ref.at[slot, k, pl.ds(i, 1)], sem.at[slot]).start(priority=k % 2)
            return carry

        lax.fori_loop(0, n_tok, issue, 0, unroll=ISSUE_UNROLL)

    @pl.when(step == 0)
    def _():
        gather(step)

    @pl.when(step + 1 < pl.num_programs(0))
    def _():
        gather(step + 1)

    slot = step % 2
    for k in range(TOP_K):
        _row_copy(y_ref.at[pl.ds(0, n_tok)], buf_ref.at[slot, k], sem.at[slot]).wait()
    f = None
    for k in range(TOP_K):
        lo, hi = _unpack_halves(buf_ref[slot, k])
        part = wt_ref[:, k:k + 1] * jnp.concatenate([lo, hi], axis=1)
        f = part if f is None else f + part
    _residual_epilogue(x_ref[...], mod_ref[0, 5:6, :], f, (nmod_ref, wg_ref, bg_ref, fin_ref), out_refs, final)


def _combine(yp, dest_flat, wt_cols, x2, mod_l, nxt, seq, final):
    t, d = x2.shape
    dh = d // 2
    nt = _tile(seq, MOVE_TOKENS)
    per_b = seq // nt
    mod_spec = pl.BlockSpec((1, 6, d), lambda i, dest: (i // per_b, 0, 0))
    row = lambda w: pl.BlockSpec((nt, w), lambda i, dest: (i, 0))
    out_specs, out_shape = _residual_out(t, d, row, final)
    return pl.pallas_call(
        functools.partial(_combine_kernel, final=final),
        grid_spec=pltpu.PrefetchScalarGridSpec(
            num_scalar_prefetch=1,
            grid=(t // nt,),
            in_specs=[pl.BlockSpec(memory_space=pl.ANY), row(TOP_K), row(d), mod_spec, mod_spec]
            + _gate_specs(d, lambda i, dest: (0, 0)) + [pl.BlockSpec((1, d), lambda i, dest: (0, 0))],
            out_specs=out_specs,
            scratch_shapes=[pltpu.VMEM((2, TOP_K, nt, dh), U32), pltpu.SemaphoreType.DMA((2,))]),
        out_shape=out_shape,
        compiler_params=_params(("arbitrary",), 40),
        name="combine",
    )(dest_flat, yp, wt_cols, x2, mod_l, *nxt)


def _dense_ffn(hp, x2, w13, w2, mod_l, nxt, seq, final):
    t = hp.shape[0]
    bm = math.gcd(t, DENSE_ROWS)
    n_tiles = t // bm
    tile_e = jnp.zeros((n_tiles,), I32)
    tile_f = jnp.full((n_tiles,), VALID, I32).at[0].set(VALID | FIRST_OF_GROUP)
    tile_next = jnp.full((n_tiles,), -1, I32)
    hid = _ffn_up(hp, w13[None], tile_e, tile_f, tile_next, bm)
    yp = _ffn_down(hid, w2[None], tile_e, tile_f, tile_next, bm)
    return _dense_residual(yp, x2, mod_l, nxt, seq, final)


def _moe_ffn(hp, logits_t, x2, w13, w2, mod_l, nxt, seq, final):
    t = hp.shape[0]
    n_e = w13.shape[0]
    idx, wts, cnt = _route(logits_t)
    counts = cnt[:, 0].astype(I32)
    padded = ((counts + FFN_ROWS - 1) // FFN_ROWS) * FFN_ROWS
    seg_end = jnp.cumsum(padded)
    seg_start = seg_end - padded
    n_tiles = (t * TOP_K) // FFN_ROWS + n_e
    tile_row0 = jnp.arange(n_tiles, dtype=I32) * FFN_ROWS
    tile_e = jnp.minimum(jnp.searchsorted(seg_end, tile_row0, side="right"), n_e - 1).astype(I32)
    valid = tile_row0 < seg_end[-1]
    first = jnp.concatenate([jnp.ones((1,), bool), tile_e[1:] != tile_e[:-1]]) & valid
    tokens_in_tile = (seg_start + counts)[tile_e] - tile_row0
    half = valid & (tokens_in_tile <= FFN_ROWS // 2)
    tile_f = (jnp.where(valid, VALID, 0) | jnp.where(first, FIRST_OF_GROUP, 0) | jnp.where(half, HALF, 0)).astype(I32)
    tile_id = jnp.arange(n_tiles, dtype=I32)
    later_start = lax.cummin(jnp.where(first, tile_id, n_tiles), reverse=True)
    next_start = jnp.concatenate([later_start[1:], jnp.full((1,), n_tiles, I32)])
    tile_next = jnp.where(next_start < n_tiles, tile_e[jnp.minimum(next_start, n_tiles - 1)], -1).astype(I32)
    start_of = jnp.sum(jnp.where(idx[:TOP_K, :, None] == jnp.arange(n_e, dtype=I32), seg_start, 0), axis=-1)
    dest = (start_of + idx[TOP_K:2 * TOP_K]).reshape(-1).astype(I32)
    xs = _dispatch(hp, dest, n_tiles * FFN_ROWS)
    hid = _ffn_up(xs, w13, tile_e, tile_f, tile_next, FFN_ROWS)
    yp = _ffn_down(hid, w2, tile_e, tile_f, tile_next, FFN_ROWS)
    return _combine(yp, dest, wts[:TOP_K].T, x2, mod_l, nxt, seq, final)


def kernel(x, c, w_mod, b_mod, w_in, b_in, w_qk_conv, m_norm_g, w_m_proj, w_dw, b_dw, ln_c_g, ln_c_b,
           w_c_proj, w_out, ffn_w13, ffn_w2, moe_router, moe_w13, moe_w2, final_g):
    bsz, seq, d = x.shape
    depth = w_mod.shape[0]
    wm = w_m_proj.shape[1]
    wc = w_dw.shape[2]
    heads = wm // HEAD_DIM
    n_gate = 4 * heads
    off_g = 4 * wm
    off_glu = off_g + n_gate
    t = bsz * seq
    assert d % (2 * LANES) == 0 and n_gate <= LANES

    mods = _adaln_mod(c, w_mod, b_mod)
    x2 = x.reshape(t, d)

    def gate_params(l):
        w_gate = jnp.zeros((d, LANES), BF16).at[:, :n_gate].set(w_in[l][:, off_g:off_glu].astype(BF16))
        b_gate = jnp.zeros((1, LANES), F32).at[0, :n_gate].set(b_in[l][off_g:off_glu])
        return w_gate, b_gate

    h, gates = _prenorm(x2, mods[0], *gate_params(0), seq)
    out = None
    for l in range(depth):
        final = l == depth - 1
        mod_l = mods[l]
        l_next = l if final else l + 1
        nxt = (mods[l_next], *gate_params(l_next), final_g.reshape(1, d))
        w_main = jnp.concatenate([w_in[l][:, :off_g], w_in[l][:, off_glu:]], axis=1).astype(BF16)
        b_main = jnp.concatenate([b_in[l][:off_g], b_in[l][off_glu:]])[None]
        z = _matmul_bias(h, w_main, b_main, BF16)
        gates = gates[:, :n_gate]
        z3 = z.reshape(bsz, seq, z.shape[1])
        k_scale = jnp.concatenate([jnp.ones((wm,), F32), jnp.full((wm,), HEAD_DIM ** -0.5, F32)])
        qk3 = _qk_conv(z3, w_qk_conv[l] * k_scale[None], 2 * wm)
        hm = _mlstm_branch(qk3, z3, gates.reshape(bsz, seq, n_gate), m_norm_g[l], heads)
        glu4 = z[:, off_g:off_g + 2 * wc].reshape(bsz, seq, 2 * wc // LANES, LANES)
        uc = _glu_conv(glu4, w_dw[l], b_dw[l])
        ln_gb = jnp.stack([ln_c_g[l], ln_c_b[l]])
        moe = l % 2 == 1
        router_t = None
        if moe:
            n_e = moe_router.shape[2]
            r_pad = jnp.zeros((d, LANES), F32).at[:, :n_e].set(moe_router[l // 2])
            r_hi = r_pad.astype(BF16)
            router_t = jnp.stack([r_hi, (r_pad - r_hi.astype(F32)).astype(BF16)])
        res = _merge(hm.reshape(t, wm), uc.reshape(t, wc), ln_gb, z, off_g + 2 * wc, x2, mod_l,
                     w_m_proj[l].astype(BF16), w_c_proj[l].astype(BF16), w_out[l].astype(BF16), seq, router_t)
        if moe:
            x2, hp, logits = res
            res = _moe_ffn(hp, logits[:, :n_e].T, x2, moe_w13[l // 2], moe_w2[l // 2], mod_l, nxt, seq, final)
        else:
            x2, hp = res
            res = _dense_ffn(hp, x2, ffn_w13[l // 2], ffn_w2[l // 2], mod_l, nxt, seq, final)
        if final:
            out = res[0]
        else:
            x2, h, gates = res
    return out.reshape(bsz, seq, d)
```
